```python
import jax, jax.numpy as jnp
from jax import lax
import numpy as np

D_MODEL = 1024
BATCH = 32
SEQ = 2048
DEPTH = 1
DEC_BATCH = 32
DEC_SEQ = 64
PAST_LEN = 1024

CHUNK = 64
GLA_HEADS = 4
GLA_DK = 64
GLA_DV = 128
GLA_LOWRANK = 16
GLA_TAU = 16.0
GLA_BLOCK = 16
ATT_HEADS = 8
ATT_KV_HEADS = 2
ATT_HD = 64
IDX_HEADS = 4
IDX_DIM = 64
TOPK_MAX = 256
D_FF = 2816
EPS = 1e-6

GLA_QK = GLA_HEADS * GLA_DK
GLA_V = GLA_HEADS * GLA_DV
ATT_Q = ATT_HEADS * ATT_HD
ATT_KV = ATT_KV_HEADS * ATT_HD
MIX_WIDTH = GLA_V + ATT_Q
IN_WIDTH = 2 * GLA_QK + GLA_V + GLA_LOWRANK + GLA_V + ATT_Q + 2 * ATT_KV + IDX_HEADS * IDX_DIM + IDX_DIM + IDX_HEADS

kernel_name = 'hybrid_gla_dsa_macaron_stream_step'


def _split_points():
    sizes = (GLA_QK, GLA_QK, GLA_V, GLA_LOWRANK, GLA_V, ATT_Q, ATT_KV, ATT_KV, IDX_HEADS * IDX_DIM, IDX_DIM, IDX_HEADS)
    pts, acc = [], 0
    for s in sizes[:-1]:
        acc += s
        pts.append(acc)
    return pts


def _alibi_slopes():
    return jnp.asarray([2.0 ** (-8.0 * (h + 1) / ATT_HEADS) for h in range(ATT_HEADS)], jnp.float32)


def _rmsnorm(x, g):
    xf = x.astype(jnp.float32)
    y = xf * lax.rsqrt(jnp.mean(xf * xf, axis=-1, keepdims=True) + EPS)
    return (y * g.astype(jnp.float32)).astype(x.dtype)


def _swiglu(h, w_gate, w_up, w_down):
    return (jax.nn.silu(h @ w_gate) * (h @ w_up)) @ w_down


def _gla_recurrence(q, k, v, log_a, s0):
    B, L, H, _ = q.shape
    DV = v.shape[-1]
    C = GLA_BLOCK
    pad = (-L) % C
    f32 = jnp.float32
    padw = ((0, 0), (0, pad), (0, 0), (0, 0))
    q, k, v, log_a = (jnp.pad(t.astype(f32), padw) for t in (q, k, v, log_a))
    n = (L + pad) // C

    def to_blocks(t):
        return t.reshape(B, n, C, H, t.shape[-1]).transpose(1, 0, 3, 2, 4)

    qb, kb, vb, gb = (to_blocks(t) for t in (q, k, v, log_a))
    bb = jnp.cumsum(gb, axis=3)
    causal = jnp.tril(jnp.ones((C, C), dtype=bool))

    def step(S, blk):
        qc, kc, vc, bc = blk
        tot = bc[:, :, -1:, :]
        q_dec = qc * jnp.exp(bc)
        k_dec = kc * jnp.exp(-bc)
        k_end = kc * jnp.exp(tot - bc)
        A = jnp.where(causal, jnp.einsum('bhtk,bhsk->bhts', q_dec, k_dec), 0.0)
        o = jnp.einsum('bhtk,bhkv->bhtv', q_dec, S) + jnp.einsum('bhts,bhsv->bhtv', A, vc)
        S = S * jnp.exp(tot)[:, :, 0, :, None] + jnp.einsum('bhsk,bhsv->bhkv', k_end, vc)
        return S, o

    sT, ob = lax.scan(step, s0.astype(f32), (qb, kb, vb, bb))
    o = ob.transpose(1, 0, 3, 2, 4).reshape(B, n * C, H, DV)[:, :L]
    return o, sT


def _gla_mixer(gq, gk, gv, ga, gr, w_a2, b_a, g_gla, s0):
    B, L, _ = gq.shape
    q = gq.reshape(B, L, GLA_HEADS, GLA_DK) * (GLA_DK ** -0.5)
    k = gk.reshape(B, L, GLA_HEADS, GLA_DK)
    v = gv.reshape(B, L, GLA_HEADS, GLA_DV)
    log_a = jax.nn.log_sigmoid((ga @ w_a2 + b_a).astype(jnp.float32)) / GLA_TAU
    log_a = log_a.reshape(B, L, GLA_HEADS, GLA_DK)
    o, sT = _gla_recurrence(q, k, v, log_a, s0)
    o = _rmsnorm(o.astype(gv.dtype), g_gla).reshape(B, L, GLA_V)
    return o * jax.nn.silu(gr), sT.astype(s0.dtype)


def _dsa_block(q, iq, iw, q_pos, k_all, v_all, ik_all, topk):
    B, Q = q.shape[:2]
    Lk = k_all.shape[1]
    f32 = jnp.float32
    k_pos = jnp.arange(Lk, dtype=jnp.int32)
    limit = (q_pos // CHUNK + 1) * CHUNK
    logits = jnp.einsum('bqhd,bld->bqhl', iq.astype(f32), ik_all)
    score = jnp.einsum('bqh,bqhl->bql', iw.astype(f32), jax.nn.relu(logits))
    admissible = k_pos[None, :] < limit[:, None]
    score = jnp.where(admissible[None], score, -jnp.inf)
    _, idx = lax.top_k(score, topk)
    sel_pos = k_pos[idx]
    sel_ok = sel_pos < limit[None, :, None]
    gather = jax.vmap(lambda rows, i: rows[i])
    kg = gather(k_all, idx).astype(f32)
    vg = gather(v_all, idx).astype(f32)
    qg = q.astype(f32).reshape(B, Q, ATT_KV_HEADS, ATT_HEADS // ATT_KV_HEADS, ATT_HD)
    s = jnp.einsum('bqgrd,bqkgd->bqgrk', qg, kg) * (ATT_HD ** -0.5)
    dist = jnp.abs(q_pos[None, :, None] - sel_pos).astype(f32)
    slopes = _alibi_slopes().reshape(ATT_KV_HEADS, ATT_HEADS // ATT_KV_HEADS)
    s = s - slopes[None, None, :, :, None] * dist[:, :, None, None, :]
    s = jnp.where(sel_ok[:, :, None, None, :], s, -jnp.inf)
    p = jax.nn.softmax(s, axis=-1)
    o = jnp.einsum('bqgrk,bqkgd->bqgrd', p, vg)
    return o.reshape(B, Q, ATT_Q).astype(q.dtype)


def _layer(x, c, past_k, past_v, past_ik, s0,
           w_ada, b_ada, g_ffn1, w1_gate, w1_up, w1_down, g_mix, w_in, w_a2, b_a, g_gla,
           g_q, g_k, w_out, g_ffn2, w2_gate, w2_up, w2_down, g_out):
    B, L, _ = x.shape
    P = past_k.shape[1]
    mod = jax.nn.silu(c) @ w_ada + b_ada
    sh1, sc1, gt1, sh2, sc2, gt2, sh3, sc3, gt3 = jnp.split(mod[:, None, :], 9, axis=-1)

    h = _rmsnorm(x, g_ffn1) * (1 + sc1) + sh1
    x = x + 0.5 * gt1 * _swiglu(h, w1_gate, w1_up, w1_down)

    h = _rmsnorm(x, g_mix) * (1 + sc2) + sh2
    z = h @ w_in
    gq, gk, gv, ga, gr, aq, ak, av, iq, ik, iw = jnp.split(z, _split_points(), axis=-1)

    gla_out, sT = _gla_mixer(gq, gk, gv, ga, gr, w_a2, b_a, g_gla, s0)

    q = _rmsnorm(aq.reshape(B, L, ATT_HEADS, ATT_HD), g_q)
    k = _rmsnorm(ak.reshape(B, L, ATT_KV_HEADS, ATT_HD), g_k)
    v = av.reshape(B, L, ATT_KV_HEADS, ATT_HD)
    iq = iq.reshape(B, L, IDX_HEADS, IDX_DIM)
    iw = iw * ((IDX_HEADS * IDX_DIM) ** -0.5)
    k_all = jnp.concatenate([past_k.astype(k.dtype), k], axis=1)
    v_all = jnp.concatenate([past_v.astype(v.dtype), v], axis=1)
    ik_all = jnp.concatenate([past_ik.astype(ik.dtype), ik], axis=1).astype(jnp.float32)
    q_pos = P + jnp.arange(L, dtype=jnp.int32)
    topk = min(TOPK_MAX, (P + L) // 4)
    if L <= CHUNK:
        att = _dsa_block(q, iq, iw, q_pos, k_all, v_all, ik_all, topk)
    else:
        n = L // CHUNK

        def to_chunks(t):
            return t.reshape(B, n, CHUNK, *t.shape[2:]).swapaxes(0, 1)

        att = lax.map(lambda a: _dsa_block(a[0], a[1], a[2], a[3], k_all, v_all, ik_all, topk),
                      (to_chunks(q), to_chunks(iq), to_chunks(iw), q_pos.reshape(n, CHUNK)))
        att = att.swapaxes(0, 1).reshape(B, L, ATT_Q)

    x = x + gt2 * (jnp.concatenate([gla_out, att], axis=-1) @ w_out)

    h = _rmsnorm(x, g_ffn2) * (1 + sc3) + sh3
    x = x + 0.5 * gt3 * _swiglu(h, w2_gate, w2_up, w2_down)
    x = _rmsnorm(x, g_out)
    return x, k, v, ik, sT


def setup_inputs(seed: int = 0) -> dict:
    key = jax.random.key(seed)
    ks = iter(jax.random.split(key, 40))
    f32 = jnp.float32

    def nrm(shape, scale):
        return jax.random.normal(next(ks), shape, f32) * scale

    def gain(shape):
        return 1.0 + 0.02 * jax.random.normal(next(ks), shape, f32)

    D = D_MODEL
    return {
        'x_prompt': nrm((BATCH, SEQ, D), 1.0),
        'x_sample': nrm((DEC_BATCH, DEC_SEQ, D), 1.0),
        'c_prompt': nrm((BATCH, D), 1.0),
        'c_sample': nrm((DEC_BATCH, D), 1.0),
        'cache_k': nrm((DEPTH, DEC_BATCH, PAST_LEN, ATT_KV_HEADS, ATT_HD), 1.0),
        'cache_v': nrm((DEPTH, DEC_BATCH, PAST_LEN, ATT_KV_HEADS, ATT_HD), 1.0),
        'cache_idx_k': nrm((DEPTH, DEC_BATCH, PAST_LEN, IDX_DIM), 1.0),
        'state_gla': nrm((DEPTH, DEC_BATCH, GLA_HEADS, GLA_DK, GLA_DV), 0.5),
        'w_ada': nrm((DEPTH, D, 9 * D), 0.5 * D ** -0.5),
        'b_ada': nrm((DEPTH, 9 * D), 0.02),
        'g_ffn1': gain((DEPTH, D)),
        'w1_gate': nrm((DEPTH, D, D_FF), D ** -0.5),
        'w1_up': nrm((DEPTH, D, D_FF), D ** -0.5),
        'w1_down': nrm((DEPTH, D_FF, D), D_FF ** -0.5),
        'g_mix': gain((DEPTH, D)),
        'w_in': nrm((DEPTH, D, IN_WIDTH), D ** -0.5),
        'w_a2': nrm((DEPTH, GLA_LOWRANK, GLA_QK), GLA_LOWRANK ** -0.5),
        'b_a': nrm((DEPTH, GLA_QK), 0.1),
        'g_gla': gain((DEPTH, GLA_DV)),
        'g_q': gain((DEPTH, ATT_HD)),
        'g_k': gain((DEPTH, ATT_HD)),
        'w_out': nrm((DEPTH, MIX_WIDTH, D), MIX_WIDTH ** -0.5),
        'g_ffn2': gain((DEPTH, D)),
        'w2_gate': nrm((DEPTH, D, D_FF), D ** -0.5),
        'w2_up': nrm((DEPTH, D, D_FF), D ** -0.5),
        'w2_down': nrm((DEPTH, D_FF, D), D_FF ** -0.5),
        'g_out': gain((DEPTH, D)),
    }


def reference(x_prompt, x_sample, c_prompt, c_sample, cache_k, cache_v, cache_idx_k, state_gla,
              w_ada, b_ada, g_ffn1, w1_gate, w1_up, w1_down, g_mix, w_in, w_a2, b_a, g_gla,
              g_q, g_k, w_out, g_ffn2, w2_gate, w2_up, w2_down, g_out):
    yp, ys = x_prompt, x_sample
    Bp = x_prompt.shape[0]
    pk_l, pv_l, pik_l, ps_l = [], [], [], []
    sk_l, sv_l, sik_l, ss_l = [], [], [], []
    empty_k = jnp.zeros((Bp, 0, ATT_KV_HEADS, ATT_HD), x_prompt.dtype)
    empty_ik = jnp.zeros((Bp, 0, IDX_DIM), x_prompt.dtype)
    zero_s = jnp.zeros((Bp, GLA_HEADS, GLA_DK, GLA_DV), state_gla.dtype)
    for l in range(DEPTH):
        params = (w_ada[l], b_ada[l], g_ffn1[l], w1_gate[l], w1_up[l], w1_down[l], g_mix[l], w_in[l],
                  w_a2[l], b_a[l], g_gla[l], g_q[l], g_k[l], w_out[l], g_ffn2[l], w2_gate[l], w2_up[l],
                  w2_down[l], g_out[l])
        yp, k1, v1, i1, s1 = _layer(yp, c_prompt, empty_k, empty_k, empty_ik, zero_s, *params)
        ys, k2, v2, i2, s2 = _layer(ys, c_sample, cache_k[l], cache_v[l], cache_idx_k[l], state_gla[l], *params)
        pk_l.append(k1); pv_l.append(v1); pik_l.append(i1); ps_l.append(s1)
        sk_l.append(k2); sv_l.append(v2); sik_l.append(i2); ss_l.append(s2)
    return (yp, ys,
            jnp.stack(pk_l), jnp.stack(pv_l), jnp.stack(pik_l), jnp.stack(ps_l),
            jnp.stack(sk_l), jnp.stack(sv_l), jnp.stack(sik_l), jnp.stack(ss_l))
```

```python
import functools

import jax
import jax.numpy as jnp
from jax import lax
from jax.experimental import pallas as pl
from jax.experimental.pallas import tpu as pltpu

F32 = jnp.float32
BF16 = jnp.bfloat16
I32 = jnp.int32

D_MODEL = 1024
D_FF = 2816
CHUNK = 64
GLA_HEADS = 4
GLA_DK = 64
GLA_DV = 128
GLA_LOWRANK = 16
GLA_TAU = 16.0
GLA_BLOCK = 16
ATT_HEADS = 8
ATT_KV_HEADS = 2
ATT_HD = 64
IDX_HEADS = 4
IDX_DIM = 64
TOPK_MAX = 256
EPS = 1e-6

GLA_QK = GLA_HEADS * GLA_DK
GLA_V = GLA_HEADS * GLA_DV
ATT_Q = ATT_HEADS * ATT_HD
ATT_KV = ATT_KV_HEADS * ATT_HD
IDX_Q = IDX_HEADS * IDX_DIM
MIX_WIDTH = GLA_V + ATT_Q
ATT_REP = ATT_HEADS // ATT_KV_HEADS

V7X_LANES = 128
V7X_SUBLANES = 8
V7X_VMEM_LIMIT_BYTES = 56 * 1024 * 1024

_SEG_GQK = 0
_SEG_GV = 512
_SEG_GR = 1024
_SEG_AQ = 1536
_SEG_AKV = 2048
_SEG_IQ = 2304
_SEG_MISC = 2560
_PACKED_WIDTH = 2688
_MISC_GA = IDX_DIM
_MISC_IW = IDX_DIM + GLA_LOWRANK
_IWT_ROWS = V7X_SUBLANES

_FF_CHUNK = 256
_GLA_T = 64
_DSA_TQ = 128
_NEG = -1e30
_INT_MIN = -(2 ** 31)

_NT = (((1,), (1,)), ((), ()))


def _dot(a, b):
    return jnp.dot(a, b, preferred_element_type=F32)


def _dot_nt(a, b):
    return lax.dot_general(a, b, _NT, preferred_element_type=F32)


def _rms(x):
    return x * lax.rsqrt(jnp.mean(x * x, axis=-1, keepdims=True) + EPS)


def _silu(x):
    return x * jax.nn.sigmoid(x)


def _row_tiling(batch, length, target):
    if length >= target:
        assert length % target == 0
        return 1, target
    nb = max(1, min(batch, target // length))
    while batch % nb:
        nb -= 1
    return nb, length


def _const_spec(shape):
    zeros = (0,) * len(shape)
    return pl.BlockSpec(shape, lambda *_: zeros, pipeline_mode=pl.Buffered(1))


def _params(*sem):
    return pltpu.CompilerParams(dimension_semantics=sem, vmem_limit_bytes=V7X_VMEM_LIMIT_BYTES)


def _adaln_kernel(c_ref, w_ref, b_ref, o_ref):
    a = _silu(c_ref[...]).astype(BF16)
    o_ref[...] = _dot(a, w_ref[...]) + b_ref[...]


def _adaln(c, w_ada, b_ada):
    bt, d = c.shape
    n = w_ada.shape[1]
    tn = d
    return pl.pallas_call(
        _adaln_kernel,
        out_shape=jax.ShapeDtypeStruct((bt, n), F32),
        grid=(n // tn,),
        in_specs=[
            pl.BlockSpec((bt, d), lambda j: (0, 0)),
            pl.BlockSpec((d, tn), lambda j: (0, j)),
            pl.BlockSpec((1, tn), lambda j: (0, j)),
        ],
        out_specs=pl.BlockSpec((bt, tn), lambda j: (0, j)),
        compiler_params=_params("arbitrary"),
        name="adaln",
    )(c, w_ada, b_ada)


def _ffn_kernel(*refs, nb, rows, mod_base, with_mix, with_final_norm):
    it = iter(refs)
    x_ref, mod_ref = next(it), next(it)
    if with_mix:
        gla_ref, att_ref, wout_ref = next(it), next(it), next(it)
    g_ref, wg_ref, wu_ref, wd_ref = next(it), next(it), next(it), next(it)
    if with_final_norm:
        gout_ref = next(it)
    o_ref, h_scr, x_scr = next(it), next(it), next(it)
    tm = nb * rows

    x = x_ref[...].reshape(tm, D_MODEL)
    if with_mix:
        gla = gla_ref[...].reshape(tm, GLA_V)
        att = att_ref[...].reshape(tm, ATT_Q)
        mix = _dot(gla, wout_ref[:GLA_V, :]) + _dot(att, wout_ref[GLA_V:, :])
    g = g_ref[...]
    for j in range(nb):
        sl = slice(j * rows, (j + 1) * rows)
        m = mod_ref[j]
        xj = x[sl]
        if with_mix:
            xj = xj + m[5:6] * mix[sl]
        x_scr[sl, :] = xj
        h = (_rms(xj) * g) * (1.0 + m[mod_base + 1:mod_base + 2]) + m[mod_base:mod_base + 1]
        h_scr[sl, :] = h.astype(BF16)

    h = h_scr[...]
    acc = jnp.zeros((tm, D_MODEL), F32)
    for c in range(D_FF // _FF_CHUNK):
        cs = slice(c * _FF_CHUNK, (c + 1) * _FF_CHUNK)
        gate = _dot(h, wg_ref[:, cs])
        up = _dot(h, wu_ref[:, cs])
        a = (_silu(gate) * up).astype(BF16)
        acc = acc + _dot(a, wd_ref[cs, :])

    for j in range(nb):
        sl = slice(j * rows, (j + 1) * rows)
        m = mod_ref[j]
        y = x_scr[sl, :] + 0.5 * m[mod_base + 2:mod_base + 3] * acc[sl]
        if with_final_norm:
            y = _rms(y) * gout_ref[...]
        o_ref[j] = y


def _ffn(x, mod, g, wg, wu, wd, *, mod_base, mix=None, g_out=None, tm=512):
    b, l, d = x.shape
    nb, rows = _row_tiling(b, l, tm)
    grid = (b // nb, l // rows)
    row_spec = lambda w: pl.BlockSpec((nb, rows, w), lambda i, r: (i, r, 0))
    in_specs = [row_spec(d), pl.BlockSpec((nb, 9, d), lambda i, r: (i, 0, 0))]
    args = [x, mod]
    if mix is not None:
        gla, att, wout = mix
        in_specs += [row_spec(GLA_V), row_spec(ATT_Q), _const_spec(wout.shape)]
        args += [gla, att, wout]
    in_specs += [_const_spec(g.shape), _const_spec(wg.shape), _const_spec(wu.shape), _const_spec(wd.shape)]
    args += [g, wg, wu, wd]
    if g_out is not None:
        in_specs.append(_const_spec(g_out.shape))
        args.append(g_out)
    kern = functools.partial(
        _ffn_kernel, nb=nb, rows=rows, mod_base=mod_base,
        with_mix=mix is not None, with_final_norm=g_out is not None)
    return pl.pallas_call(
        kern,
        out_shape=jax.ShapeDtypeStruct((b, l, d), F32),
        grid=grid,
        in_specs=in_specs,
        out_specs=row_spec(d),
        scratch_shapes=[pltpu.VMEM((nb * rows, d), BF16), pltpu.VMEM((nb * rows, d), F32)],
        compiler_params=_params("parallel", "parallel"),
        name="ffn_mix" if mix is not None else "ffn",
    )(*args)


def _proj_kernel(x_ref, mod_ref, g_ref, w_ref, wa2_ref, ba_ref, gq_ref, gk_ref, bd_ref,
                 qk_ref, v_ref, sgr_ref, la_ref, qatt_ref, knew_ref, vnew_ref, vtnew_ref,
                 iknew_ref, iq_ref, iwt_ref, h_scr, *, nb, rows):
    tm = nb * rows
    x = x_ref[...].reshape(tm, D_MODEL)
    g = g_ref[...]
    for j in range(nb):
        sl = slice(j * rows, (j + 1) * rows)
        m = mod_ref[j]
        h = (_rms(x[sl]) * g) * (1.0 + m[4:5]) + m[3:4]
        h_scr[sl, :] = h.astype(BF16)
    h = h_scr[...]

    def seg(start, width):
        return _dot(h, w_ref[:, start:start + width])

    def put(ref, val, width):
        ref[...] = val.reshape(nb, rows, width).astype(ref.dtype)

    lane = lax.broadcasted_iota(I32, (1, 2 * GLA_QK), 1)
    qscale = jnp.where(lane < GLA_QK, GLA_DK ** -0.5, 1.0).astype(F32)
    put(qk_ref, seg(_SEG_GQK, 2 * GLA_QK) * qscale, 2 * GLA_QK)
    put(v_ref, seg(_SEG_GV, GLA_V), GLA_V)
    put(sgr_ref, _silu(seg(_SEG_GR, GLA_V)), GLA_V)

    aq = seg(_SEG_AQ, ATT_Q)
    msq = _dot((aq * aq).astype(BF16), bd_ref[...]) * (1.0 / ATT_HD)
    put(qatt_ref, aq * lax.rsqrt(msq + EPS) * gq_ref[...] * (ATT_HD ** -0.5), ATT_Q)

    akv = seg(_SEG_AKV, 2 * ATT_KV)
    ak, av = akv[:, :ATT_KV], akv[:, ATT_KV:]
    msk = _dot((ak * ak).astype(BF16), bd_ref[:ATT_KV, :ATT_KV]) * (1.0 / ATT_HD)
    put(knew_ref, ak * lax.rsqrt(msk + EPS) * gk_ref[...], ATT_KV)
    put(vnew_ref, av, ATT_KV)
    avt = av.T
    for j in range(nb):
        vtnew_ref[j] = avt[:, j * rows:(j + 1) * rows].astype(BF16)

    put(iq_ref, seg(_SEG_IQ, IDX_Q), IDX_Q)

    misc = seg(_SEG_MISC, V7X_LANES)
    put(iknew_ref, misc[:, :IDX_DIM], IDX_DIM)
    za = _dot(misc.astype(BF16), wa2_ref[...]) + ba_ref[...]
    log_sig = jnp.minimum(za, 0.0) - jnp.log(1.0 + jnp.exp(-jnp.abs(za)))
    put(la_ref, log_sig * (1.0 / GLA_TAU), GLA_QK)
    misct = misc.T
    iwt = misct[_MISC_IW:_MISC_IW + _IWT_ROWS, :] * ((IDX_HEADS * IDX_DIM) ** -0.5)
    for j in range(nb):
        iwt_ref[j] = iwt[:, j * rows:(j + 1) * rows]


def _proj(x, mod, g_mix, w_packed, wa2, b_a, gq_t, gk_t, bd, *, tm=512):
    b, l, d = x.shape
    nb, rows = _row_tiling(b, l, tm)
    grid = (b // nb, l // rows)
    row_spec = lambda w: pl.BlockSpec((nb, rows, w), lambda i, r: (i, r, 0))
    col_spec = lambda h: pl.BlockSpec((nb, h, rows), lambda i, r: (i, 0, r))
    sds = lambda w, dt: jax.ShapeDtypeStruct((b, l, w), dt)
    out_shape = [
        sds(2 * GLA_QK, F32), sds(GLA_V, BF16), sds(GLA_V, BF16), sds(GLA_QK, F32), sds(ATT_Q, BF16),
        sds(ATT_KV, F32), sds(ATT_KV, F32), jax.ShapeDtypeStruct((b, ATT_KV, l), BF16),
        sds(IDX_DIM, F32), sds(IDX_Q, F32), jax.ShapeDtypeStruct((b, _IWT_ROWS, l), F32),
    ]
    out_specs = [
        row_spec(2 * GLA_QK), row_spec(GLA_V), row_spec(GLA_V), row_spec(GLA_QK), row_spec(ATT_Q),
        row_spec(ATT_KV), row_spec(ATT_KV), col_spec(ATT_KV),
        row_spec(IDX_DIM), row_spec(IDX_Q), col_spec(_IWT_ROWS),
    ]
    consts = [g_mix, w_packed, wa2, b_a, gq_t, gk_t, bd]
    return pl.pallas_call(
        functools.partial(_proj_kernel, nb=nb, rows=rows),
        out_shape=out_shape,
        grid=grid,
        in_specs=[row_spec(d), pl.BlockSpec((nb, 9, d), lambda i, r: (i, 0, 0))]
        + [_const_spec(c.shape) for c in consts],
        out_specs=out_specs,
        scratch_shapes=[pltpu.VMEM((nb * rows, d), BF16)],
        compiler_params=_params("parallel", "parallel"),
        name="proj",
    )(x, mod, *consts)


def _gla_kernel(*refs, has_s0):
    it = iter(refs)
    qk_ref, v_ref, la_ref, sgr_ref = next(it), next(it), next(it), next(it)
    s0_ref = next(it) if has_s0 else None
    g_ref, bd_ref, o_ref, st_ref, s_scr = next(it), next(it), next(it), next(it), next(it)
    t, cb = _GLA_T, GLA_BLOCK
    nsb = t // cb
    c = pl.program_id(1)

    @pl.when(c == 0)
    def _():
        s_scr[...] = jnp.zeros((GLA_QK, GLA_V), F32)
        if has_s0:
            for h in range(GLA_HEADS):
                s_scr[h * GLA_DK:(h + 1) * GLA_DK, h * GLA_DV:(h + 1) * GLA_DV] = s0_ref[0, h]

    qk = qk_ref[0]
    q, k = qk[:, :GLA_QK], qk[:, GLA_QK:]
    v = v_ref[0]
    la = la_ref[0]

    ri = lax.broadcasted_iota(I32, (t, t), 0)
    ci = lax.broadcasted_iota(I32, (t, t), 1)
    tril = jnp.where(ri >= ci, 1.0, 0.0).astype(BF16)
    p0 = la.astype(BF16)
    r0 = la - p0.astype(F32)
    p1 = r0.astype(BF16)
    p2 = (r0 - p1.astype(F32)).astype(BF16)
    b = _dot(tril, p0) + _dot(tril, p1) + _dot(tril, p2)

    b0 = [jnp.zeros((1, GLA_QK), F32)] + [b[cb * i - 1:cb * i, :] for i in range(1, nsb)]
    btot = b[t - 1:t, :]
    bstart = jnp.concatenate([jnp.broadcast_to(b0[i], (cb, GLA_QK)) for i in range(nsb)], axis=0)
    q_rel = q * jnp.exp(b - bstart)
    q_int = (q * jnp.exp(b)).astype(BF16)
    k_end = k * jnp.exp(btot - b)

    s_old = s_scr[...]
    o = _dot(q_int, s_old.astype(BF16))

    lane = lax.broadcasted_iota(I32, (1, GLA_QK), 1)
    head_mask = [jnp.where((lane >= h * GLA_DK) & (lane < (h + 1) * GLA_DK), 1.0, 0.0).astype(F32)
                 for h in range(GLA_HEADS)]
    intra = []
    for i in range(nsb):
        n = cb * (i + 1)
        qs = q_rel[cb * i:cb * (i + 1)]
        q_stack = jnp.concatenate([qs * head_mask[h] for h in range(GLA_HEADS)], axis=0).astype(BF16)
        km = (k[:n] * jnp.exp(b0[i] - b[:n])).astype(BF16)
        a = _dot_nt(q_stack, km)
        rr = lax.broadcasted_iota(I32, (GLA_HEADS * cb, n), 0)
        cc = lax.broadcasted_iota(I32, (GLA_HEADS * cb, n), 1)
        a = jnp.where(cc - cb * i <= (rr & (cb - 1)), a, 0.0).astype(BF16)
        oi = _dot(a, v[:n])
        intra.append(jnp.concatenate(
            [oi[h * cb:(h + 1) * cb, h * GLA_DV:(h + 1) * GLA_DV] for h in range(GLA_HEADS)], axis=1))
    o = o + jnp.concatenate(intra, axis=0)

    g = g_ref[...]
    normed = []
    for h in range(GLA_HEADS):
        oh = o[:, h * GLA_DV:(h + 1) * GLA_DV]
        normed.append(_rms(oh) * g)
    on = jnp.concatenate(normed, axis=1)
    o_ref[0] = (on * sgr_ref[0].astype(F32)).astype(o_ref.dtype)

    ds = _dot(k_end.T.astype(BF16), v)
    dcol = jnp.exp(jnp.broadcast_to(btot, (V7X_SUBLANES, GLA_QK)).T[:, 0:1])
    s_new = s_old * dcol + ds * bd_ref[...]
    s_scr[...] = s_new

    @pl.when(c == pl.num_programs(1) - 1)
    def _():
        for h in range(GLA_HEADS):
            st_ref[0, h] = s_new[h * GLA_DK:(h + 1) * GLA_DK, h * GLA_DV:(h + 1) * GLA_DV]


def _gla(qk, v, la, sgr, s0, g_gla, bd_state):
    b, l, _ = qk.shape
    assert l % _GLA_T == 0
    tok = lambda w: pl.BlockSpec((1, _GLA_T, w), lambda i, c: (i, c, 0))
    st_spec = pl.BlockSpec((1, GLA_HEADS, GLA_DK, GLA_DV), lambda i, c: (i, 0, 0, 0))
    in_specs = [tok(2 * GLA_QK), tok(GLA_V), tok(GLA_QK), tok(GLA_V)]
    args = [qk, v, la, sgr]
    if s0 is not None:
        in_specs.append(st_spec)
        args.append(s0)
    in_specs += [_const_spec(g_gla.shape), _const_spec(bd_state.shape)]
    args += [g_gla, bd_state]
    return pl.pallas_call(
        functools.partial(_gla_kernel, has_s0=s0 is not None),
        out_shape=[jax.ShapeDtypeStruct((b, l, GLA_V), BF16),
                   jax.ShapeDtypeStruct((b, GLA_HEADS, GLA_DK, GLA_DV), F32)],
        grid=(b, l // _GLA_T),
        in_specs=in_specs,
        out_specs=[tok(GLA_V), st_spec],
        scratch_shapes=[pltpu.VMEM((GLA_QK, GLA_V), F32)],
        compiler_params=_params("parallel", "arbitrary"),
        name="gla",
    )(*args)


def _dsa_kernel(*refs, p_len, e_new, tq, t0, topk, nbits):
    has_past = p_len > 0
    it = iter(refs)
    if has_past:
        kp_ref, vp_ref, ikp_ref = next(it), next(it), next(it)
    kn_ref, vtn_ref, ikn_ref, q_ref, iq_ref, iwt_ref = (next(it) for _ in range(6))
    o_ref, kg_scr, vt_scr, ik3_scr, keys_scr, aux_scr, bias_scr = (next(it) for _ in range(7))
    e = p_len + e_new
    t = pl.program_id(1)

    @pl.when(t == 0)
    def _():
        def put(off, n, k32, vt16, ik32):
            for g in range(ATT_KV_HEADS):
                kg_scr[g, off:off + n, :] = k32[:, g * ATT_HD:(g + 1) * ATT_HD].astype(BF16)
            vt_scr[:, off:off + n] = vt16
            hi = ik32.astype(BF16)
            lo = (ik32 - hi.astype(F32)).astype(BF16)
            ik3_scr[off:off + n, :] = jnp.concatenate([hi, lo, hi], axis=1)

        if has_past:
            put(0, p_len, kp_ref[0], vp_ref[0].T.astype(BF16), ikp_ref[0])
        put(p_len, e_new, kn_ref[0], vtn_ref[0], ikn_ref[0])

    row = lax.broadcasted_iota(I32, (e, tq), 0)
    qpos = p_len + (t0 + t) * tq + lax.broadcasted_iota(I32, (1, tq), 1)
    limit = ((qpos >> 6) + 1) << 6
    adm = row < limit

    iq = iq_ref[0]
    iwt = iwt_ref[0]
    ik3 = ik3_scr[...]
    score = jnp.zeros((e, tq), F32)
    for h in range(IDX_HEADS):
        iqh = iq[:, h * IDX_DIM:(h + 1) * IDX_DIM]
        hi = iqh.astype(BF16)
        lo = (iqh - hi.astype(F32)).astype(BF16)
        logit = _dot_nt(ik3, jnp.concatenate([hi, hi, lo], axis=1))
        score = score + iwt[h:h + 1, :] * jnp.maximum(logit, 0.0)
    score = jnp.where(score == 0.0, 0.0, score)
    bits = lax.bitcast_convert_type(score, I32)
    key = bits ^ ((bits >> 31) & 0x7FFFFFFF)
    keys_scr[...] = jnp.where(adm, key, _INT_MIN)

    def count_ge(cand):
        return jnp.sum(jnp.where(keys_scr[...] >= cand, 1, 0), axis=0, keepdims=True)

    thr = jnp.where(count_ge(jnp.zeros((1, tq), I32)) >= topk, 0, _INT_MIN).astype(I32)

    def thr_step(i, thr):
        cand = thr | lax.shift_left(jnp.int32(1), 30 - i)
        return jnp.where(count_ge(cand) >= topk, cand, thr)

    thr = lax.fori_loop(0, 31, thr_step, thr)

    keys = keys_scr[...]
    need = topk - jnp.sum(jnp.where(keys > thr, 1, 0), axis=0, keepdims=True)
    aux_scr[...] = jnp.where(keys == thr, row, jnp.int32(2 ** 30))

    def tie_step(i, x):
        cand = x | lax.shift_left(jnp.int32(1), nbits - 1 - i)
        below = jnp.sum(jnp.where(aux_scr[...] < cand, 1, 0), axis=0, keepdims=True)
        return jnp.where(below < need, cand, x)

    x = lax.fori_loop(0, nbits, tie_step, jnp.zeros((1, tq), I32))
    sel_bias = jnp.where(keys > thr, 0.0, jnp.where(aux_scr[...] <= x, 0.0, _NEG))
    bias_scr[...] = jnp.where(adm, sel_bias, _NEG)

    dist = jnp.abs(qpos - row).astype(F32)
    q = q_ref[0]
    outs = []
    for h in range(ATT_HEADS):
        g = h // ATT_REP
        slope = 2.0 ** (-8.0 * (h + 1) / ATT_HEADS)
        s = _dot_nt(kg_scr[g], q[:, h * ATT_HD:(h + 1) * ATT_HD])
        s = s - slope * dist + bias_scr[...]
        m = jnp.max(s, axis=0, keepdims=True)
        p = jnp.exp(s - m)
        denom = jnp.sum(p, axis=0, keepdims=True)
        ot = _dot(vt_scr[g * ATT_HD:(g + 1) * ATT_HD, :], p.astype(BF16))
        outs.append(ot / denom)
    ot = jnp.concatenate(outs, axis=0)
    if tq % V7X_LANES:
        ot = jnp.concatenate([ot, jnp.zeros((ATT_Q, V7X_LANES - tq % V7X_LANES), F32)], axis=1)
    o_ref[0] = ot.T[:tq].astype(o_ref.dtype)


def _dsa_call(q_att, iq, iwt, k_new, vt_new, ik_new, past, *, tq, t0, n_tiles, topk):
    b = q_att.shape[0]
    p_len = 0 if past is None else past[0].shape[1]
    e_new = (t0 + n_tiles) * tq
    e = p_len + e_new
    in_specs, args = [], []
    if past is not None:
        in_specs += [pl.BlockSpec((1, p_len, w), lambda i, t: (i, 0, 0)) for w in (ATT_KV, ATT_KV, IDX_DIM)]
        args += list(past)
    in_specs += [
        pl.BlockSpec((1, e_new, ATT_KV), lambda i, t: (i, 0, 0)),
        pl.BlockSpec((1, ATT_KV, e_new), lambda i, t: (i, 0, 0)),
        pl.BlockSpec((1, e_new, IDX_DIM), lambda i, t: (i, 0, 0)),
        pl.BlockSpec((1, tq, ATT_Q), lambda i, t: (i, t0 + t, 0)),
        pl.BlockSpec((1, tq, IDX_Q), lambda i, t: (i, t0 + t, 0)),
        pl.BlockSpec((1, _IWT_ROWS, tq), lambda i, t: (i, 0, t0 + t)),
    ]
    args += [k_new, vt_new, ik_new, q_att, iq, iwt]
    kern = functools.partial(_dsa_kernel, p_len=p_len, e_new=e_new, tq=tq, t0=t0, topk=topk,
                             nbits=max(1, (e - 1).bit_length()))
    return pl.pallas_call(
        kern,
        out_shape=jax.ShapeDtypeStruct((b, n_tiles * tq, ATT_Q), BF16),
        grid=(b, n_tiles),
        in_specs=in_specs,
        out_specs=pl.BlockSpec((1, tq, ATT_Q), lambda i, t: (i, t, 0)),
        scratch_shapes=[
            pltpu.VMEM((ATT_KV_HEADS, e, ATT_HD), BF16),
            pltpu.VMEM((ATT_KV, e), BF16),
            pltpu.VMEM((e, 3 * IDX_DIM), BF16),
            pltpu.VMEM((e, tq), I32),
            pltpu.VMEM((e, tq), I32),
            pltpu.VMEM((e, tq), F32),
        ],
        compiler_params=_params("parallel", "arbitrary"),
        name=f"dsa_e{e}",
    )(*args)


def _dsa(q_att, iq, iwt, k_new, vt_new, ik_new, past, *, n_classes=4):
    b, l, _ = q_att.shape
    p_len = 0 if past is None else past[0].shape[1]
    topk = min(TOPK_MAX, (p_len + l) // 4)
    if l <= CHUNK:
        return _dsa_call(q_att, iq, iwt, k_new, vt_new, ik_new, past, tq=l, t0=0, n_tiles=1, topk=topk)
    assert l % _DSA_TQ == 0
    n_tiles = l // _DSA_TQ
    n_cls = min(n_classes, n_tiles)
    while n_tiles % n_cls:
        n_cls -= 1
    per = n_tiles // n_cls
    outs = [_dsa_call(q_att, iq, iwt, k_new, vt_new, ik_new, past, tq=_DSA_TQ, t0=c * per, n_tiles=per, topk=topk)
            for c in range(n_cls)]
    return jnp.concatenate(outs, axis=1)


def _block_diag_ones(n, blk):
    i = jnp.arange(n) // blk
    return (i[:, None] == i[None, :])


def _prep_weights(w_ada, b_ada, g_ffn1, w1_gate, w1_up, w1_down, g_mix, w_in, w_a2, b_a, g_gla,
                  g_q, g_k, w_out, g_ffn2, w2_gate, w2_up, w2_down, g_out):
    sizes = (GLA_QK, GLA_QK, GLA_V, GLA_LOWRANK, GLA_V, ATT_Q, ATT_KV, ATT_KV, IDX_Q, IDX_DIM, IDX_HEADS)
    pts, acc = [], 0
    for s in sizes[:-1]:
        acc += s
        pts.append(acc)
    gq, gk, gv, ga, gr, aq, ak, av, iq, ik, iw = jnp.split(w_in, pts, axis=1)
    pad = jnp.zeros((D_MODEL, _PACKED_WIDTH - sum(sizes)), w_in.dtype)
    w_packed = jnp.concatenate([gq, gk, gv, gr, aq, ak, av, iq, ik, ga, iw, pad], axis=1).astype(BF16)
    wa2 = jnp.zeros((V7X_LANES, GLA_QK), F32).at[_MISC_GA:_MISC_GA + GLA_LOWRANK].set(w_a2).astype(BF16)
    row = lambda a: a.reshape(1, -1).astype(F32)
    bd_state = jnp.repeat(jnp.repeat(jnp.eye(GLA_HEADS, dtype=F32), GLA_DK, axis=0), GLA_DV, axis=1)
    return dict(
        w_ada=w_ada.astype(BF16), b_ada=row(b_ada),
        g_ffn1=row(g_ffn1), w1=(w1_gate.astype(BF16), w1_up.astype(BF16), w1_down.astype(BF16)),
        g_mix=row(g_mix), w_packed=w_packed, wa2=wa2, b_a=row(b_a),
        g_gla=row(g_gla), gq_t=row(jnp.tile(g_q, ATT_HEADS)), gk_t=row(jnp.tile(g_k, ATT_KV_HEADS)),
        bd_heads=_block_diag_ones(ATT_Q, ATT_HD).astype(BF16), bd_state=bd_state,
        w_out=w_out.astype(BF16),
        g_ffn2=row(g_ffn2), w2=(w2_gate.astype(BF16), w2_up.astype(BF16), w2_down.astype(BF16)),
        g_out=row(g_out),
    )


def _layer(x, mod, past, s0, w):
    b, l, _ = x.shape
    x1 = _ffn(x, mod, w["g_ffn1"], *w["w1"], mod_base=0)
    (qk, gv, sgr, la, q_att, k_new, v_new, vt_new, ik_new, iq, iwt) = _proj(
        x1, mod, w["g_mix"], w["w_packed"], w["wa2"], w["b_a"], w["gq_t"], w["gk_t"], w["bd_heads"])
    gla_out, s_t = _gla(qk, gv, la, sgr, s0, w["g_gla"], w["bd_state"])
    att = _dsa(q_att, iq, iwt, k_new, vt_new, ik_new, past)
    y = _ffn(x1, mod, w["g_ffn2"], *w["w2"], mod_base=6, mix=(gla_out, att, w["w_out"]), g_out=w["g_out"])
    kv_shape = (b, l, ATT_KV_HEADS, ATT_HD)
    return y, k_new.reshape(kv_shape), v_new.reshape(kv_shape), ik_new, s_t


def kernel(x_prompt, x_sample, c_prompt, c_sample, cache_k, cache_v, cache_idx_k, state_gla, w_ada, b_ada, g_ffn1, w1_gate, w1_up, w1_down, g_mix, w_in, w_a2, b_a, g_gla, g_q, g_k, w_out, g_ffn2, w2_gate, w2_up, w2_down, g_out):
    depth = w_ada.shape[0]
    bp = x_prompt.shape[0]
    yp, ys = x_prompt, x_sample
    outs_p, outs_s = [], []
    for layer in range(depth):
        w = _prep_weights(*(t[layer] for t in (
            w_ada, b_ada, g_ffn1, w1_gate, w1_up, w1_down, g_mix, w_in, w_a2, b_a, g_gla,
            g_q, g_k, w_out, g_ffn2, w2_gate, w2_up, w2_down, g_out)))
        mod = _adaln(jnp.concatenate([c_prompt, c_sample], axis=0), w["w_ada"], w["b_ada"])
        mod = mod.reshape(mod.shape[0], 9, D_MODEL)
        ds, pp = cache_k.shape[1], cache_k.shape[2]
        past = (cache_k[layer].reshape(ds, pp, ATT_KV), cache_v[layer].reshape(ds, pp, ATT_KV), cache_idx_k[layer])
        yp, *rest_p = _layer(yp, mod[:bp], None, None, w)
        ys, *rest_s = _layer(ys, mod[bp:], past, state_gla[layer], w)
        outs_p.append(rest_p)
        outs_s.append(rest_s)
    stack = lambda outs, i: jnp.stack([o[i] for o in outs])
    return (yp, ys,
            stack(outs_p, 0), stack(outs_p, 1), stack(outs_p, 2), stack(outs_p, 3),
            stack(outs_s, 0), stack(outs_s, 1), stack(outs_s, 2), stack(outs_s, 3))
```

```python
import functools

import jax
import jax.numpy as jnp
from jax import lax
from jax.experimental import pallas as pl
from jax.experimental.pallas import tpu as pltpu

F32 = jnp.float32
BF16 = jnp.bfloat16
I32 = jnp.int32

D_MODEL = 1024
D_FF = 2816
CHUNK = 64
GLA_HEADS = 4
GLA_DK = 64
GLA_DV = 128
GLA_LOWRANK = 16
GLA_TAU = 16.0
GLA_BLOCK = 16
ATT_HEADS = 8
ATT_KV_HEADS = 2
ATT_HD = 64
IDX_HEADS = 4
IDX_DIM = 64
TOPK_MAX = 256
EPS = 1e-6

GLA_QK = GLA_HEADS * GLA_DK
GLA_V = GLA_HEADS * GLA_DV
ATT_Q = ATT_HEADS * ATT_HD
ATT_KV = ATT_KV_HEADS * ATT_HD
IDX_Q = IDX_HEADS * IDX_DIM
MIX_WIDTH = GLA_V + ATT_Q
ATT_REP = ATT_HEADS // ATT_KV_HEADS

V7X_LANES = 128
V7X_SUBLANES = 8
V7X_VMEM_LIMIT_BYTES = 56 * 1024 * 1024

_SEG_GQK = 0
_SEG_GV = 512
_SEG_GR = 1024
_SEG_AQ = 1536
_SEG_AKV = 2048
_SEG_IQ = 2304
_SEG_MISC = 2560
_PACKED_WIDTH = 2688
_MISC_GA = IDX_DIM
_MISC_IW = IDX_DIM + GLA_LOWRANK
_IWT_ROWS = V7X_SUBLANES

_FF_CHUNK = 256
_GLA_T = 64
_DSA_TQ = 128
_NEG = -1e30
_INT_MIN = -(2 ** 31)
_POS_BIG = 2 ** 30
_CHUNK_SHIFT = CHUNK.bit_length() - 1
assert 1 << _CHUNK_SHIFT == CHUNK

_NT = (((1,), (1,)), ((), ()))


def _dot(a, b):
    return jnp.dot(a, b, preferred_element_type=F32)


def _dot_nt(a, b):
    return lax.dot_general(a, b, _NT, preferred_element_type=F32)


def _rms(x):
    return x * lax.rsqrt(jnp.mean(x * x, axis=-1, keepdims=True) + EPS)


def _silu(x):
    return x * jax.nn.sigmoid(x)


def _col_reduce(x, op, chains=8):
    r, n = x.shape
    while chains > 1 and r % (chains * V7X_SUBLANES):
        chains //= 2
    if r % (chains * V7X_SUBLANES):
        return op(x, axis=0, keepdims=True)
    slab = chains * V7X_SUBLANES
    pair = jnp.add if op is jnp.sum else jnp.maximum
    acc = x[:slab]
    for i in range(1, r // slab):
        acc = pair(acc, x[i * slab:(i + 1) * slab])
    return op(acc, axis=0, keepdims=True)


def _row_tiling(batch, length, target):
    if length >= target:
        assert length % target == 0
        return 1, target
    nb = max(1, min(batch, target // length))
    while batch % nb:
        nb -= 1
    return nb, length


def _const_spec(shape):
    zeros = (0,) * len(shape)
    return pl.BlockSpec(shape, lambda *_: zeros, pipeline_mode=pl.Buffered(1))


def _params(*sem):
    return pltpu.CompilerParams(dimension_semantics=sem, vmem_limit_bytes=V7X_VMEM_LIMIT_BYTES)


def _adaln_kernel(c_ref, w_ref, b_ref, o_ref):
    a = _silu(c_ref[...]).astype(BF16)
    o_ref[...] = _dot(a, w_ref[...]) + b_ref[...]


def _adaln(c, w_ada, b_ada):
    bt, d = c.shape
    n = w_ada.shape[1]
    tn = d
    return pl.pallas_call(
        _adaln_kernel,
        out_shape=jax.ShapeDtypeStruct((bt, n), F32),
        grid=(n // tn,),
        in_specs=[
            pl.BlockSpec((bt, d), lambda j: (0, 0)),
            pl.BlockSpec((d, tn), lambda j: (0, j)),
            pl.BlockSpec((1, tn), lambda j: (0, j)),
        ],
        out_specs=pl.BlockSpec((bt, tn), lambda j: (0, j)),
        compiler_params=_params("arbitrary"),
        name="adaln",
    )(c, w_ada, b_ada)


def _ffn_kernel(*refs, nb, rows, mod_base, with_mix, with_final_norm):
    it = iter(refs)
    x_ref, mod_ref = next(it), next(it)
    if with_mix:
        gla_ref, att_ref, wout_ref = next(it), next(it), next(it)
    g_ref, wg_ref, wu_ref, wd_ref = next(it), next(it), next(it), next(it)
    if with_final_norm:
        gout_ref = next(it)
    o_ref, h_scr, x_scr = next(it), next(it), next(it)
    tm = nb * rows

    x = x_ref[...].reshape(tm, D_MODEL)
    if with_mix:
        gla = gla_ref[...].reshape(tm, GLA_V)
        att = att_ref[...].reshape(tm, ATT_Q)
        mix = _dot(gla, wout_ref[:GLA_V, :]) + _dot(att, wout_ref[GLA_V:, :])
    g = g_ref[...]
    for j in range(nb):
        sl = slice(j * rows, (j + 1) * rows)
        m = mod_ref[j]
        xj = x[sl]
        if with_mix:
            xj = xj + m[5:6] * mix[sl]
        x_scr[sl, :] = xj
        h = (_rms(xj) * g) * (1.0 + m[mod_base + 1:mod_base + 2]) + m[mod_base:mod_base + 1]
        h_scr[sl, :] = h.astype(BF16)

    h = h_scr[...]
    acc = jnp.zeros((tm, D_MODEL), F32)
    for c in range(D_FF // _FF_CHUNK):
        cs = slice(c * _FF_CHUNK, (c + 1) * _FF_CHUNK)
        gate = _dot(h, wg_ref[:, cs])
        up = _dot(h, wu_ref[:, cs])
        a = (_silu(gate) * up).astype(BF16)
        acc = acc + _dot(a, wd_ref[cs, :])

    for j in range(nb):
        sl = slice(j * rows, (j + 1) * rows)
        m = mod_ref[j]
        y = x_scr[sl, :] + 0.5 * m[mod_base + 2:mod_base + 3] * acc[sl]
        if with_final_norm:
            y = _rms(y) * gout_ref[...]
        o_ref[j] = y


def _ffn(x, mod, g, wg, wu, wd, *, mod_base, mix=None, g_out=None, tm=512):
    b, l, d = x.shape
    nb, rows = _row_tiling(b, l, tm)
    grid = (b // nb, l // rows)
    row_spec = lambda w: pl.BlockSpec((nb, rows, w), lambda i, r: (i, r, 0))
    in_specs = [row_spec(d), pl.BlockSpec((nb, 9, d), lambda i, r: (i, 0, 0))]
    args = [x, mod]
    if mix is not None:
        gla, att, wout = mix
        in_specs += [row_spec(GLA_V), row_spec(ATT_Q), _const_spec(wout.shape)]
        args += [gla, att, wout]
    in_specs += [_const_spec(g.shape), _const_spec(wg.shape), _const_spec(wu.shape), _const_spec(wd.shape)]
    args += [g, wg, wu, wd]
    if g_out is not None:
        in_specs.append(_const_spec(g_out.shape))
        args.append(g_out)
    kern = functools.partial(
        _ffn_kernel, nb=nb, rows=rows, mod_base=mod_base,
        with_mix=mix is not None, with_final_norm=g_out is not None)
    return pl.pallas_call(
        kern,
        out_shape=jax.ShapeDtypeStruct((b, l, d), F32),
        grid=grid,
        in_specs=in_specs,
        out_specs=row_spec(d),
        scratch_shapes=[pltpu.VMEM((nb * rows, d), BF16), pltpu.VMEM((nb * rows, d), F32)],
        compiler_params=_params("parallel", "parallel"),
        name="ffn_mix" if mix is not None else "ffn",
    )(*args)


def _proj_kernel(x_ref, mod_ref, g_ref, w_ref, wa2_ref, ba_ref, gq_ref, gk_ref, bd_ref,
                 qk_ref, v_ref, sgr_ref, la_ref, qatt_ref, knew_ref, vnew_ref, vtnew_ref,
                 iknew_ref, iq_ref, iwt_ref, h_scr, *, nb, rows):
    tm = nb * rows
    x = x_ref[...].reshape(tm, D_MODEL)
    g = g_ref[...]
    for j in range(nb):
        sl = slice(j * rows, (j + 1) * rows)
        m = mod_ref[j]
        h = (_rms(x[sl]) * g) * (1.0 + m[4:5]) + m[3:4]
        h_scr[sl, :] = h.astype(BF16)
    h = h_scr[...]

    def seg(start, width):
        return _dot(h, w_ref[:, start:start + width])

    def put(ref, val, width):
        ref[...] = val.reshape(nb, rows, width).astype(ref.dtype)

    lane = lax.broadcasted_iota(I32, (1, 2 * GLA_QK), 1)
    qscale = jnp.where(lane < GLA_QK, GLA_DK ** -0.5, 1.0).astype(F32)
    put(qk_ref, seg(_SEG_GQK, 2 * GLA_QK) * qscale, 2 * GLA_QK)
    put(v_ref, seg(_SEG_GV, GLA_V), GLA_V)
    put(sgr_ref, _silu(seg(_SEG_GR, GLA_V)), GLA_V)

    aq = seg(_SEG_AQ, ATT_Q)
    msq = _dot((aq * aq).astype(BF16), bd_ref[...]) * (1.0 / ATT_HD)
    put(qatt_ref, aq * lax.rsqrt(msq + EPS) * gq_ref[...] * (ATT_HD ** -0.5), ATT_Q)

    akv = seg(_SEG_AKV, 2 * ATT_KV)
    ak, av = akv[:, :ATT_KV], akv[:, ATT_KV:]
    msk = _dot((ak * ak).astype(BF16), bd_ref[:ATT_KV, :ATT_KV]) * (1.0 / ATT_HD)
    put(knew_ref, ak * lax.rsqrt(msk + EPS) * gk_ref[...], ATT_KV)
    put(vnew_ref, av, ATT_KV)
    avt = av.T
    for j in range(nb):
        vtnew_ref[j] = avt[:, j * rows:(j + 1) * rows].astype(BF16)

    put(iq_ref, seg(_SEG_IQ, IDX_Q), IDX_Q)

    misc = seg(_SEG_MISC, V7X_LANES)
    put(iknew_ref, misc[:, :IDX_DIM], IDX_DIM)
    za = _dot(misc.astype(BF16), wa2_ref[...]) + ba_ref[...]
    log_sig = jnp.minimum(za, 0.0) - jnp.log(1.0 + jnp.exp(-jnp.abs(za)))
    put(la_ref, log_sig * (1.0 / GLA_TAU), GLA_QK)
    misct = misc.T
    iwt = misct[_MISC_IW:_MISC_IW + _IWT_ROWS, :] * ((IDX_HEADS * IDX_DIM) ** -0.5)
    for j in range(nb):
        iwt_ref[j] = iwt[:, j * rows:(j + 1) * rows]


def _proj(x, mod, g_mix, w_packed, wa2, b_a, gq_t, gk_t, bd, *, tm=512):
    b, l, d = x.shape
    nb, rows = _row_tiling(b, l, tm)
    grid = (b // nb, l // rows)
    row_spec = lambda w: pl.BlockSpec((nb, rows, w), lambda i, r: (i, r, 0))
    col_spec = lambda h: pl.BlockSpec((nb, h, rows), lambda i, r: (i, 0, r))
    sds = lambda w, dt: jax.ShapeDtypeStruct((b, l, w), dt)
    out_shape = [
        sds(2 * GLA_QK, F32), sds(GLA_V, BF16), sds(GLA_V, BF16), sds(GLA_QK, F32), sds(ATT_Q, BF16),
        sds(ATT_KV, F32), sds(ATT_KV, F32), jax.ShapeDtypeStruct((b, ATT_KV, l), BF16),
        sds(IDX_DIM, F32), sds(IDX_Q, F32), jax.ShapeDtypeStruct((b, _IWT_ROWS, l), F32),
    ]
    out_specs = [
        row_spec(2 * GLA_QK), row_spec(GLA_V), row_spec(GLA_V), row_spec(GLA_QK), row_spec(ATT_Q),
        row_spec(ATT_KV), row_spec(ATT_KV), col_spec(ATT_KV),
        row_spec(IDX_DIM), row_spec(IDX_Q), col_spec(_IWT_ROWS),
    ]
    consts = [g_mix, w_packed, wa2, b_a, gq_t, gk_t, bd]
    return pl.pallas_call(
        functools.partial(_proj_kernel, nb=nb, rows=rows),
        out_shape=out_shape,
        grid=grid,
        in_specs=[row_spec(d), pl.BlockSpec((nb, 9, d), lambda i, r: (i, 0, 0))]
        + [_const_spec(c.shape) for c in consts],
        out_specs=out_specs,
        scratch_shapes=[pltpu.VMEM((nb * rows, d), BF16)],
        compiler_params=_params("parallel", "parallel"),
        name="proj",
    )(x, mod, *consts)


def _gla_kernel(*refs, has_s0, nbb):
    it = iter(refs)
    qk_ref, v_ref, la_ref, sgr_ref = next(it), next(it), next(it), next(it)
    s0_ref = next(it) if has_s0 else None
    g_ref, bd_ref, o_ref, st_ref, s_scr = next(it), next(it), next(it), next(it), next(it)
    t, cb = _GLA_T, GLA_BLOCK
    nsb = t // cb
    c = pl.program_id(1)

    @pl.when(c == 0)
    def _():
        s_scr[...] = jnp.zeros((nbb, GLA_QK, GLA_V), F32)
        if has_s0:
            for j in range(nbb):
                for h in range(GLA_HEADS):
                    s_scr[j, h * GLA_DK:(h + 1) * GLA_DK, h * GLA_DV:(h + 1) * GLA_DV] = s0_ref[j, h]

    ri = lax.broadcasted_iota(I32, (t, t), 0)
    ci = lax.broadcasted_iota(I32, (t, t), 1)
    tril = jnp.where(ri >= ci, 1.0, 0.0).astype(BF16)
    lane = lax.broadcasted_iota(I32, (1, GLA_QK), 1)
    head_mask = [jnp.where((lane >= h * GLA_DK) & (lane < (h + 1) * GLA_DK), 1.0, 0.0).astype(F32)
                 for h in range(GLA_HEADS)]
    g = g_ref[...]


    cum = []
    for j in range(nbb):
        la = la_ref[j]
        p0 = la.astype(BF16)
        r0 = la - p0.astype(F32)
        p1 = r0.astype(BF16)
        p2 = (r0 - p1.astype(F32)).astype(BF16)
        bb = _dot(tril, jnp.concatenate([p0, p1, p2], axis=1))
        cum.append(bb[:, :GLA_QK] + bb[:, GLA_QK:2 * GLA_QK] + bb[:, 2 * GLA_QK:])

    o_inter, scores = [], []
    for j in range(nbb):
        b = cum[j]
        qk = qk_ref[j]
        q, k = qk[:, :GLA_QK], qk[:, GLA_QK:]
        v = v_ref[j]
        b0 = [jnp.zeros((1, GLA_QK), F32)] + [b[cb * i - 1:cb * i, :] for i in range(1, nsb)]
        btot = b[t - 1:t, :]
        bstart = jnp.concatenate([jnp.broadcast_to(b0[i], (cb, GLA_QK)) for i in range(nsb)], axis=0)
        q_rel = q * jnp.exp(b - bstart)
        q_int = (q * jnp.exp(b)).astype(BF16)
        k_end = k * jnp.exp(btot - b)

        s_old = s_scr[j]
        o_inter.append(_dot(q_int, s_old.astype(BF16)))
        ds = _dot(k_end.T.astype(BF16), v)
        dcol = jnp.exp(jnp.broadcast_to(btot, (V7X_SUBLANES, GLA_QK)).T[:, 0:1])
        s_scr[j] = s_old * dcol + ds * bd_ref[...]

        sc = []
        for i in range(nsb):
            n = cb * (i + 1)
            qs = q_rel[cb * i:cb * (i + 1)]
            q_stack = jnp.concatenate([qs * head_mask[h] for h in range(GLA_HEADS)], axis=0).astype(BF16)
            km = (k[:n] * jnp.exp(b0[i] - b[:n])).astype(BF16)
            sc.append(_dot_nt(q_stack, km))
        scores.append(sc)

    for j in range(nbb):
        v = v_ref[j]
        intra = []
        for i in range(nsb):
            n = cb * (i + 1)
            rr = lax.broadcasted_iota(I32, (GLA_HEADS * cb, n), 0)
            cc = lax.broadcasted_iota(I32, (GLA_HEADS * cb, n), 1)
            a = jnp.where(cc - cb * i <= (rr & (cb - 1)), scores[j][i], 0.0).astype(BF16)
            oi = _dot(a, v[:n])
            intra.append(jnp.concatenate(
                [oi[h * cb:(h + 1) * cb, h * GLA_DV:(h + 1) * GLA_DV] for h in range(GLA_HEADS)], axis=1))
        o = o_inter[j] + jnp.concatenate(intra, axis=0)
        on = jnp.concatenate([_rms(o[:, h * GLA_DV:(h + 1) * GLA_DV]) * g for h in range(GLA_HEADS)], axis=1)
        o_ref[j] = (on * sgr_ref[j].astype(F32)).astype(o_ref.dtype)

    @pl.when(c == pl.num_programs(1) - 1)
    def _():
        for j in range(nbb):
            for h in range(GLA_HEADS):
                st_ref[j, h] = s_scr[j, h * GLA_DK:(h + 1) * GLA_DK, h * GLA_DV:(h + 1) * GLA_DV]


def _gla(qk, v, la, sgr, s0, g_gla, bd_state, *, nbb=4):
    b, l, _ = qk.shape
    assert l % _GLA_T == 0
    while b % nbb:
        nbb -= 1
    tok = lambda w: pl.BlockSpec((nbb, _GLA_T, w), lambda i, c: (i, c, 0))
    st_spec = pl.BlockSpec((nbb, GLA_HEADS, GLA_DK, GLA_DV), lambda i, c: (i, 0, 0, 0))
    in_specs = [tok(2 * GLA_QK), tok(GLA_V), tok(GLA_QK), tok(GLA_V)]
    args = [qk, v, la, sgr]
    if s0 is not None:
        in_specs.append(st_spec)
        args.append(s0)
    in_specs += [_const_spec(g_gla.shape), _const_spec(bd_state.shape)]
    args += [g_gla, bd_state]
    return pl.pallas_call(
        functools.partial(_gla_kernel, has_s0=s0 is not None, nbb=nbb),
        out_shape=[jax.ShapeDtypeStruct((b, l, GLA_V), BF16),
                   jax.ShapeDtypeStruct((b, GLA_HEADS, GLA_DK, GLA_DV), F32)],
        grid=(b // nbb, l // _GLA_T),
        in_specs=in_specs,
        out_specs=[tok(GLA_V), st_spec],
        scratch_shapes=[pltpu.VMEM((nbb, GLA_QK, GLA_V), F32)],
        compiler_params=_params("parallel", "arbitrary"),
        name="gla",
    )(*args)


def _dsa_kernel(*refs, p_len, e_new, tq, t0, topk, nbits):
    has_past = p_len > 0
    it = iter(refs)
    if has_past:
        kp_ref, vp_ref, ikp_ref = next(it), next(it), next(it)
    kn_ref, vtn_ref, ikn_ref, q_ref, iq_ref, iwt_ref = (next(it) for _ in range(6))
    o_ref, kg_scr, vt_scr, ik3_scr, score_scr, aux_scr, bias_scr, x_scr = (next(it) for _ in range(8))
    e = p_len + e_new
    t = pl.program_id(1)

    @pl.when(t == 0)
    def _():
        def put(off, n, k32, vt16, ik32):
            for g in range(ATT_KV_HEADS):
                kg_scr[g, off:off + n, :] = k32[:, g * ATT_HD:(g + 1) * ATT_HD].astype(BF16)
            vt_scr[:, off:off + n] = vt16
            hi = ik32.astype(BF16)
            lo = (ik32 - hi.astype(F32)).astype(BF16)
            ik3_scr[off:off + n, :] = jnp.concatenate([hi, lo, hi], axis=1)

        if has_past:
            put(0, p_len, kp_ref[0], vp_ref[0].T.astype(BF16), ikp_ref[0])
        put(p_len, e_new, kn_ref[0], vtn_ref[0], ikn_ref[0])

    row = lax.broadcasted_iota(I32, (e, tq), 0)
    qpos = p_len + (t0 + t) * tq + lax.broadcasted_iota(I32, (1, tq), 1)
    limit = ((qpos >> _CHUNK_SHIFT) + 1) << _CHUNK_SHIFT
    adm = row < limit

    iq = iq_ref[0]
    iwt = iwt_ref[0]
    ik3 = ik3_scr[...]
    score = jnp.zeros((e, tq), F32)
    for h in range(IDX_HEADS):
        iqh = iq[:, h * IDX_DIM:(h + 1) * IDX_DIM]
        hi = iqh.astype(BF16)
        lo = (iqh - hi.astype(F32)).astype(BF16)
        logit = _dot_nt(ik3, jnp.concatenate([hi, hi, lo], axis=1))
        score = score + iwt[h:h + 1, :] * jnp.maximum(logit, 0.0)
    score_scr[...] = jnp.where(adm, score, -jnp.inf)

    def key_to_f32(c):
        return lax.bitcast_convert_type(c ^ ((c >> 31) & 0x7FFFFFFF), F32)

    def count_ge(cand):
        return _col_reduce(jnp.where(score_scr[...] >= key_to_f32(cand), 1.0, 0.0), jnp.sum)

    c0 = count_ge(jnp.zeros((1, tq), I32))
    thr = jnp.where(c0 >= topk, 0, _INT_MIN).astype(I32)
    cnt = jnp.where(c0 >= topk, c0, float(e))

    def thr_step(i, carry):
        thr, cnt = carry
        cand = thr | lax.shift_left(jnp.int32(1), 30 - i)
        c = count_ge(cand)
        ok = c >= topk
        return jnp.where(ok, cand, thr), jnp.where(ok, c, cnt)

    thr, cnt = lax.fori_loop(0, 31, thr_step, (thr, cnt))
    thr_f = jnp.where(thr == _INT_MIN, -jnp.inf, key_to_f32(thr))

    x_scr[...] = jnp.full((1, tq), _POS_BIG, I32)

    @pl.when(jnp.max(cnt) > topk)
    def _():
        sc = score_scr[...]
        need = topk - _col_reduce(jnp.where(sc > thr_f, 1.0, 0.0), jnp.sum)
        aux_scr[...] = jnp.where(sc == thr_f, row, _POS_BIG)

        def tie_step(i, x):
            cand = x | lax.shift_left(jnp.int32(1), nbits - 1 - i)
            below = _col_reduce(jnp.where(aux_scr[...] < cand, 1.0, 0.0), jnp.sum)
            return jnp.where(below < need, cand, x)

        x_scr[...] = lax.fori_loop(0, nbits, tie_step, jnp.zeros((1, tq), I32))

    sc = score_scr[...]
    sel_bias = jnp.where(sc > thr_f, 0.0, jnp.where(sc == thr_f, jnp.where(row <= x_scr[...], 0.0, _NEG), _NEG))
    bias_scr[...] = jnp.where(adm, sel_bias, _NEG)

    dist = jnp.abs(qpos - row).astype(F32)
    q = q_ref[0]
    def qk_logits(h):
        return _dot_nt(kg_scr[h // ATT_REP], q[:, h * ATT_HD:(h + 1) * ATT_HD])

    outs = []
    s_next = qk_logits(0)
    for h in range(ATT_HEADS):
        g = h // ATT_REP
        slope = 2.0 ** (-8.0 * (h + 1) / ATT_HEADS)
        s = s_next
        if h + 1 < ATT_HEADS:
            s_next = qk_logits(h + 1)
        s = s - slope * dist + bias_scr[...]
        m = _col_reduce(s, jnp.max)
        p = jnp.exp(s - m)
        denom = _col_reduce(p, jnp.sum)
        ot = _dot(vt_scr[g * ATT_HD:(g + 1) * ATT_HD, :], p.astype(BF16))
        outs.append(ot / denom)
    ot = jnp.concatenate(outs, axis=0)
    if tq % V7X_LANES:
        ot = jnp.concatenate([ot, jnp.zeros((ATT_Q, V7X_LANES - tq % V7X_LANES), F32)], axis=1)
    o_ref[0] = ot.T[:tq].astype(o_ref.dtype)


def _dsa_call(q_att, iq, iwt, k_new, vt_new, ik_new, past, *, tq, t0, n_tiles, topk):
    b = q_att.shape[0]
    p_len = 0 if past is None else past[0].shape[1]
    e_new = (t0 + n_tiles) * tq
    e = p_len + e_new
    in_specs, args = [], []
    if past is not None:
        in_specs += [pl.BlockSpec((1, p_len, w), lambda i, t: (i, 0, 0)) for w in (ATT_KV, ATT_KV, IDX_DIM)]
        args += list(past)
    in_specs += [
        pl.BlockSpec((1, e_new, ATT_KV), lambda i, t: (i, 0, 0)),
        pl.BlockSpec((1, ATT_KV, e_new), lambda i, t: (i, 0, 0)),
        pl.BlockSpec((1, e_new, IDX_DIM), lambda i, t: (i, 0, 0)),
        pl.BlockSpec((1, tq, ATT_Q), lambda i, t: (i, t0 + t, 0)),
        pl.BlockSpec((1, tq, IDX_Q), lambda i, t: (i, t0 + t, 0)),
        pl.BlockSpec((1, _IWT_ROWS, tq), lambda i, t: (i, 0, t0 + t)),
    ]
    args += [k_new, vt_new, ik_new, q_att, iq, iwt]
    kern = functools.partial(_dsa_kernel, p_len=p_len, e_new=e_new, tq=tq, t0=t0, topk=topk,
                             nbits=max(1, (e - 1).bit_length()))
    return pl.pallas_call(
        kern,
        out_shape=jax.ShapeDtypeStruct((b, n_tiles * tq, ATT_Q), BF16),
        grid=(b, n_tiles),
        in_specs=in_specs,
        out_specs=pl.BlockSpec((1, tq, ATT_Q), lambda i, t: (i, t, 0)),
        scratch_shapes=[
            pltpu.VMEM((ATT_KV_HEADS, e, ATT_HD), BF16),
            pltpu.VMEM((ATT_KV, e), BF16),
            pltpu.VMEM((e, 3 * IDX_DIM), BF16),
            pltpu.VMEM((e, tq), F32),
            pltpu.VMEM((e, tq), I32),
            pltpu.VMEM((e, tq), F32),
            pltpu.VMEM((1, tq), I32),
        ],
        compiler_params=_params("parallel", "arbitrary"),
        name=f"dsa_e{e}",
    )(*args)


def _dsa(q_att, iq, iwt, k_new, vt_new, ik_new, past, *, n_classes=4):
    b, l, _ = q_att.shape
    p_len = 0 if past is None else past[0].shape[1]
    topk = min(TOPK_MAX, (p_len + l) // 4)
    if l <= CHUNK:
        return _dsa_call(q_att, iq, iwt, k_new, vt_new, ik_new, past, tq=l, t0=0, n_tiles=1, topk=topk)
    assert l % _DSA_TQ == 0
    n_tiles = l // _DSA_TQ
    n_cls = min(n_classes, n_tiles)
    while n_tiles % n_cls:
        n_cls -= 1
    per = n_tiles // n_cls
    outs = [_dsa_call(q_att, iq, iwt, k_new, vt_new, ik_new, past, tq=_DSA_TQ, t0=c * per, n_tiles=per, topk=topk)
            for c in range(n_cls)]
    return jnp.concatenate(outs, axis=1)


def _block_diag_ones(n, blk):
    i = jnp.arange(n) // blk
    return (i[:, None] == i[None, :])


def _prep_weights(w_ada, b_ada, g_ffn1, w1_gate, w1_up, w1_down, g_mix, w_in, w_a2, b_a, g_gla,
                  g_q, g_k, w_out, g_ffn2, w2_gate, w2_up, w2_down, g_out):
    sizes = (GLA_QK, GLA_QK, GLA_V, GLA_LOWRANK, GLA_V, ATT_Q, ATT_KV, ATT_KV, IDX_Q, IDX_DIM, IDX_HEADS)
    pts, acc = [], 0
    for s in sizes[:-1]:
        acc += s
        pts.append(acc)
    gq, gk, gv, ga, gr, aq, ak, av, iq, ik, iw = jnp.split(w_in, pts, axis=1)
    pad = jnp.zeros((D_MODEL, _PACKED_WIDTH - sum(sizes)), w_in.dtype)
    w_packed = jnp.concatenate([gq, gk, gv, gr, aq, ak, av, iq, ik, ga, iw, pad], axis=1).astype(BF16)
    wa2 = jnp.zeros((V7X_LANES, GLA_QK), F32).at[_MISC_GA:_MISC_GA + GLA_LOWRANK].set(w_a2).astype(BF16)
    row = lambda a: a.reshape(1, -1).astype(F32)
    bd_state = jnp.repeat(jnp.repeat(jnp.eye(GLA_HEADS, dtype=F32), GLA_DK, axis=0), GLA_DV, axis=1)
    return dict(
        w_ada=w_ada.astype(BF16), b_ada=row(b_ada),
        g_ffn1=row(g_ffn1), w1=(w1_gate.astype(BF16), w1_up.astype(BF16), w1_down.astype(BF16)),
        g_mix=row(g_mix), w_packed=w_packed, wa2=wa2, b_a=row(b_a),
        g_gla=row(g_gla), gq_t=row(jnp.tile(g_q, ATT_HEADS)), gk_t=row(jnp.tile(g_k, ATT_KV_HEADS)),
        bd_heads=_block_diag_ones(ATT_Q, ATT_HD).astype(BF16), bd_state=bd_state,
        w_out=w_out.astype(BF16),
        g_ffn2=row(g_ffn2), w2=(w2_gate.astype(BF16), w2_up.astype(BF16), w2_down.astype(BF16)),
        g_out=row(g_out),
    )


def _layer(x, mod, past, s0, w):
    b, l, _ = x.shape
    x1 = _ffn(x, mod, w["g_ffn1"], *w["w1"], mod_base=0)
    (qk, gv, sgr, la, q_att, k_new, v_new, vt_new, ik_new, iq, iwt) = _proj(
        x1, mod, w["g_mix"], w["w_packed"], w["wa2"], w["b_a"], w["gq_t"], w["gk_t"], w["bd_heads"])
    gla_out, s_t = _gla(qk, gv, la, sgr, s0, w["g_gla"], w["bd_state"])
    att = _dsa(q_att, iq, iwt, k_new, vt_new, ik_new, past)
    y = _ffn(x1, mod, w["g_ffn2"], *w["w2"], mod_base=6, mix=(gla_out, att, w["w_out"]), g_out=w["g_out"])
    kv_shape = (b, l, ATT_KV_HEADS, ATT_HD)
    return y, k_new.reshape(kv_shape), v_new.reshape(kv_shape), ik_new, s_t


def kernel(x_prompt, x_sample, c_prompt, c_sample, cache_k, cache_v, cache_idx_k, state_gla, w_ada, b_ada, g_ffn1, w1_gate, w1_up, w1_down, g_mix, w_in, w_a2, b_a, g_gla, g_q, g_k, w_out, g_ffn2, w2_gate, w2_up, w2_down, g_out):
    depth = w_ada.shape[0]
    bp = x_prompt.shape[0]
    yp, ys = x_prompt, x_sample
    outs_p, outs_s = [], []
    for layer in range(depth):
        w = _prep_weights(*(t[layer] for t in (
            w_ada, b_ada, g_ffn1, w1_gate, w1_up, w1_down, g_mix, w_in, w_a2, b_a, g_gla,
            g_q, g_k, w_out, g_ffn2, w2_gate, w2_up, w2_down, g_out)))
        mod = _adaln(jnp.concatenate([c_prompt, c_sample], axis=0), w["w_ada"], w["b_ada"])
        mod = mod.reshape(mod.shape[0], 9, D_MODEL)
        ds, pp = cache_k.shape[1], cache_k.shape[2]
        past = (cache_k[layer].reshape(ds, pp, ATT_KV), cache_v[layer].reshape(ds, pp, ATT_KV), cache_idx_k[layer])
        yp, *rest_p = _layer(yp, mod[:bp], None, None, w)
        ys, *rest_s = _layer(ys, mod[bp:], past, state_gla[layer], w)
        outs_p.append(rest_p)
        outs_s.append(rest_s)
    stack = lambda outs, i: jnp.stack([o[i] for o in outs])
    return (yp, ys,
            stack(outs_p, 0), stack(outs_p, 1), stack(outs_p, 2), stack(outs_p, 3),
            stack(outs_s, 0), stack(outs_s, 1), stack(outs_s, 2), stack(outs_s, 3))
```

```python
import functools

import jax
import jax.numpy as jnp
from jax import lax
from jax.experimental import pallas as pl
from jax.experimental.pallas import tpu as pltpu

F32 = jnp.float32
BF16 = jnp.bfloat16
I32 = jnp.int32

D_MODEL = 1024
D_FF = 2816
CHUNK = 64
GLA_HEADS = 4
GLA_DK = 64
GLA_DV = 128
GLA_LOWRANK = 16
GLA_TAU = 16.0
GLA_BLOCK = 16
ATT_HEADS = 8
ATT_KV_HEADS = 2
ATT_HD = 64
IDX_HEADS = 4
IDX_DIM = 64
TOPK_MAX = 256
EPS = 1e-6

GLA_QK = GLA_HEADS * GLA_DK
GLA_V = GLA_HEADS * GLA_DV
ATT_Q = ATT_HEADS * ATT_HD
ATT_KV = ATT_KV_HEADS * ATT_HD
IDX_Q = IDX_HEADS * IDX_DIM
MIX_WIDTH = GLA_V + ATT_Q
ATT_REP = ATT_HEADS // ATT_KV_HEADS

V7X_LANES = 128
V7X_SUBLANES = 8
V7X_VMEM_LIMIT_BYTES = 56 * 1024 * 1024

_SEG_GQK = 0
_SEG_GV = 512
_SEG_GR = 1024
_SEG_AQ = 1536
_SEG_AKV = 2048
_SEG_IQ = 2304
_SEG_MISC = 2560
_PACKED_WIDTH = 2688
_MISC_GA = IDX_DIM
_MISC_IW = IDX_DIM + GLA_LOWRANK
_IWT_ROWS = V7X_SUBLANES

_FF_CHUNK = 256
_GLA_T = 64
_DSA_TQ = 128
_NEG = -1e30
_INT_MIN = -(2 ** 31)
_POS_BIG = 2 ** 30
_CHUNK_SHIFT = CHUNK.bit_length() - 1
_POS_SPLIT = 256
assert 1 << _CHUNK_SHIFT == CHUNK

_NT = (((1,), (1,)), ((), ()))


def _dot(a, b):
    return jnp.dot(a, b, preferred_element_type=F32)


def _dot_nt(a, b):
    return lax.dot_general(a, b, _NT, preferred_element_type=F32)


def _rms(x):
    return x * lax.rsqrt(jnp.mean(x * x, axis=-1, keepdims=True) + EPS)


def _silu(x):
    return x * jax.nn.sigmoid(x)


def _col_reduce(x, op, chains=8):
    r, n = x.shape
    while chains > 1 and r % (chains * V7X_SUBLANES):
        chains //= 2
    if r % (chains * V7X_SUBLANES):
        return op(x, axis=0, keepdims=True)
    slab = chains * V7X_SUBLANES
    pair = jnp.add if op is jnp.sum else jnp.maximum
    acc = x[:slab]
    for i in range(1, r // slab):
        acc = pair(acc, x[i * slab:(i + 1) * slab])
    return op(acc, axis=0, keepdims=True)


def _row_tiling(batch, length, target):
    if length >= target:
        assert length % target == 0
        return 1, target
    nb = max(1, min(batch, target // length))
    while batch % nb:
        nb -= 1
    return nb, length


def _const_spec(shape):
    zeros = (0,) * len(shape)
    return pl.BlockSpec(shape, lambda *_: zeros, pipeline_mode=pl.Buffered(1))


def _params(*sem):
    return pltpu.CompilerParams(dimension_semantics=sem, vmem_limit_bytes=V7X_VMEM_LIMIT_BYTES)


def _adaln_kernel(c_ref, w_ref, b_ref, o_ref):
    a = _silu(c_ref[...]).astype(BF16)
    o_ref[...] = _dot(a, w_ref[...]) + b_ref[...]


def _adaln(c, w_ada, b_ada):
    bt, d = c.shape
    n = w_ada.shape[1]
    tn = d
    return pl.pallas_call(
        _adaln_kernel,
        out_shape=jax.ShapeDtypeStruct((bt, n), F32),
        grid=(n // tn,),
        in_specs=[
            pl.BlockSpec((bt, d), lambda j: (0, 0)),
            pl.BlockSpec((d, tn), lambda j: (0, j)),
            pl.BlockSpec((1, tn), lambda j: (0, j)),
        ],
        out_specs=pl.BlockSpec((bt, tn), lambda j: (0, j)),
        compiler_params=_params("arbitrary"),
        name="adaln",
    )(c, w_ada, b_ada)


def _ffn_kernel(*refs, nb, rows, mod_base, with_mix, with_final_norm):
    it = iter(refs)
    x_ref, mod_ref = next(it), next(it)
    if with_mix:
        gla_ref, att_ref, wout_ref = next(it), next(it), next(it)
    g_ref, wg_ref, wu_ref, wd_ref = next(it), next(it), next(it), next(it)
    if with_final_norm:
        gout_ref = next(it)
    o_ref, h_scr, x_scr = next(it), next(it), next(it)
    tm = nb * rows

    x = x_ref[...].reshape(tm, D_MODEL)
    if with_mix:
        gla = gla_ref[...].reshape(tm, GLA_V)
        att = att_ref[...].reshape(tm, ATT_Q)
        mix = _dot(gla, wout_ref[:GLA_V, :]) + _dot(att, wout_ref[GLA_V:, :])
    g = g_ref[...]
    for j in range(nb):
        sl = slice(j * rows, (j + 1) * rows)
        m = mod_ref[j]
        xj = x[sl]
        if with_mix:
            xj = xj + m[5:6] * mix[sl]
        x_scr[sl, :] = xj
        h = (_rms(xj) * g) * (1.0 + m[mod_base + 1:mod_base + 2]) + m[mod_base:mod_base + 1]
        h_scr[sl, :] = h.astype(BF16)

    h = h_scr[...]
    acc = jnp.zeros((tm, D_MODEL), F32)
    for c in range(D_FF // _FF_CHUNK):
        cs = slice(c * _FF_CHUNK, (c + 1) * _FF_CHUNK)
        gate = _dot(h, wg_ref[:, cs])
        up = _dot(h, wu_ref[:, cs])
        a = (_silu(gate) * up).astype(BF16)
        acc = acc + _dot(a, wd_ref[cs, :])

    for j in range(nb):
        sl = slice(j * rows, (j + 1) * rows)
        m = mod_ref[j]
        y = x_scr[sl, :] + 0.5 * m[mod_base + 2:mod_base + 3] * acc[sl]
        if with_final_norm:
            y = _rms(y) * gout_ref[...]
        o_ref[j] = y


def _ffn(x, mod, g, wg, wu, wd, *, mod_base, mix=None, g_out=None, tm=512):
    b, l, d = x.shape
    nb, rows = _row_tiling(b, l, tm)
    grid = (b // nb, l // rows)
    row_spec = lambda w: pl.BlockSpec((nb, rows, w), lambda i, r: (i, r, 0))
    in_specs = [row_spec(d), pl.BlockSpec((nb, 9, d), lambda i, r: (i, 0, 0))]
    args = [x, mod]
    if mix is not None:
        gla, att, wout = mix
        in_specs += [row_spec(GLA_V), row_spec(ATT_Q), _const_spec(wout.shape)]
        args += [gla, att, wout]
    in_specs += [_const_spec(g.shape), _const_spec(wg.shape), _const_spec(wu.shape), _const_spec(wd.shape)]
    args += [g, wg, wu, wd]
    if g_out is not None:
        in_specs.append(_const_spec(g_out.shape))
        args.append(g_out)
    kern = functools.partial(
        _ffn_kernel, nb=nb, rows=rows, mod_base=mod_base,
        with_mix=mix is not None, with_final_norm=g_out is not None)
    return pl.pallas_call(
        kern,
        out_shape=jax.ShapeDtypeStruct((b, l, d), F32),
        grid=grid,
        in_specs=in_specs,
        out_specs=row_spec(d),
        scratch_shapes=[pltpu.VMEM((nb * rows, d), BF16), pltpu.VMEM((nb * rows, d), F32)],
        compiler_params=_params("parallel", "parallel"),
        name="ffn_mix" if mix is not None else "ffn",
    )(*args)


def _proj_kernel(x_ref, mod_ref, g_ref, w_ref, wa2_ref, ba_ref, gq_ref, gk_ref, bd_ref,
                 qk_ref, v_ref, sgr_ref, la_ref, qatt_ref, knew_ref, vnew_ref, vtnew_ref,
                 iknew_ref, iq_ref, iwt_ref, h_scr, *, nb, rows):
    tm = nb * rows
    x = x_ref[...].reshape(tm, D_MODEL)
    g = g_ref[...]
    for j in range(nb):
        sl = slice(j * rows, (j + 1) * rows)
        m = mod_ref[j]
        h = (_rms(x[sl]) * g) * (1.0 + m[4:5]) + m[3:4]
        h_scr[sl, :] = h.astype(BF16)
    h = h_scr[...]

    def seg(start, width):
        return _dot(h, w_ref[:, start:start + width])

    def put(ref, val, width):
        ref[...] = val.reshape(nb, rows, width).astype(ref.dtype)

    lane = lax.broadcasted_iota(I32, (1, 2 * GLA_QK), 1)
    qscale = jnp.where(lane < GLA_QK, GLA_DK ** -0.5, 1.0).astype(F32)
    put(qk_ref, seg(_SEG_GQK, 2 * GLA_QK) * qscale, 2 * GLA_QK)
    put(v_ref, seg(_SEG_GV, GLA_V), GLA_V)
    put(sgr_ref, _silu(seg(_SEG_GR, GLA_V)), GLA_V)

    aq = seg(_SEG_AQ, ATT_Q)
    msq = _dot((aq * aq).astype(BF16), bd_ref[...]) * (1.0 / ATT_HD)
    put(qatt_ref, aq * lax.rsqrt(msq + EPS) * gq_ref[...] * (ATT_HD ** -0.5), ATT_Q)

    akv = seg(_SEG_AKV, 2 * ATT_KV)
    ak, av = akv[:, :ATT_KV], akv[:, ATT_KV:]
    msk = _dot((ak * ak).astype(BF16), bd_ref[:ATT_KV, :ATT_KV]) * (1.0 / ATT_HD)
    put(knew_ref, ak * lax.rsqrt(msk + EPS) * gk_ref[...], ATT_KV)
    put(vnew_ref, av, ATT_KV)
    avt = av.T
    for j in range(nb):
        vtnew_ref[j] = avt[:, j * rows:(j + 1) * rows].astype(BF16)

    put(iq_ref, seg(_SEG_IQ, IDX_Q), IDX_Q)

    misc = seg(_SEG_MISC, V7X_LANES)
    put(iknew_ref, misc[:, :IDX_DIM], IDX_DIM)
    za = _dot(misc.astype(BF16), wa2_ref[...]) + ba_ref[...]
    log_sig = jnp.minimum(za, 0.0) - jnp.log(1.0 + jnp.exp(-jnp.abs(za)))
    put(la_ref, log_sig * (1.0 / GLA_TAU), GLA_QK)
    misct = misc.T
    iwt = misct[_MISC_IW:_MISC_IW + _IWT_ROWS, :] * ((IDX_HEADS * IDX_DIM) ** -0.5)
    for j in range(nb):
        iwt_ref[j] = iwt[:, j * rows:(j + 1) * rows]


def _proj(x, mod, g_mix, w_packed, wa2, b_a, gq_t, gk_t, bd, *, tm=512):
    b, l, d = x.shape
    nb, rows = _row_tiling(b, l, tm)
    grid = (b // nb, l // rows)
    row_spec = lambda w: pl.BlockSpec((nb, rows, w), lambda i, r: (i, r, 0))
    col_spec = lambda h: pl.BlockSpec((nb, h, rows), lambda i, r: (i, 0, r))
    sds = lambda w, dt: jax.ShapeDtypeStruct((b, l, w), dt)
    out_shape = [
        sds(2 * GLA_QK, F32), sds(GLA_V, BF16), sds(GLA_V, BF16), sds(GLA_QK, F32), sds(ATT_Q, BF16),
        sds(ATT_KV, F32), sds(ATT_KV, F32), jax.ShapeDtypeStruct((b, ATT_KV, l), BF16),
        sds(IDX_DIM, F32), sds(IDX_Q, F32), jax.ShapeDtypeStruct((b, _IWT_ROWS, l), F32),
    ]
    out_specs = [
        row_spec(2 * GLA_QK), row_spec(GLA_V), row_spec(GLA_V), row_spec(GLA_QK), row_spec(ATT_Q),
        row_spec(ATT_KV), row_spec(ATT_KV), col_spec(ATT_KV),
        row_spec(IDX_DIM), row_spec(IDX_Q), col_spec(_IWT_ROWS),
    ]
    consts = [g_mix, w_packed, wa2, b_a, gq_t, gk_t, bd]
    return pl.pallas_call(
        functools.partial(_proj_kernel, nb=nb, rows=rows),
        out_shape=out_shape,
        grid=grid,
        in_specs=[row_spec(d), pl.BlockSpec((nb, 9, d), lambda i, r: (i, 0, 0))]
        + [_const_spec(c.shape) for c in consts],
        out_specs=out_specs,
        scratch_shapes=[pltpu.VMEM((nb * rows, d), BF16)],
        compiler_params=_params("parallel", "parallel"),
        name="proj",
    )(x, mod, *consts)


def _gla_kernel(*refs, has_s0, nbb):
    it = iter(refs)
    qk_ref, v_ref, la_ref, sgr_ref = next(it), next(it), next(it), next(it)
    s0_ref = next(it) if has_s0 else None
    g_ref, bd_ref, o_ref, st_ref, s_scr = next(it), next(it), next(it), next(it), next(it)
    t, cb = _GLA_T, GLA_BLOCK
    nsb = t // cb
    c = pl.program_id(1)

    @pl.when(c == 0)
    def _():
        s_scr[...] = jnp.zeros((nbb, GLA_QK, GLA_V), F32)
        if has_s0:
            for j in range(nbb):
                for h in range(GLA_HEADS):
                    s_scr[j, h * GLA_DK:(h + 1) * GLA_DK, h * GLA_DV:(h + 1) * GLA_DV] = s0_ref[j, h]

    ri = lax.broadcasted_iota(I32, (t, t), 0)
    ci = lax.broadcasted_iota(I32, (t, t), 1)
    tril = jnp.where(ri >= ci, 1.0, 0.0).astype(BF16)
    lane = lax.broadcasted_iota(I32, (1, GLA_QK), 1)
    head_mask = [jnp.where((lane >= h * GLA_DK) & (lane < (h + 1) * GLA_DK), 1.0, 0.0).astype(F32)
                 for h in range(GLA_HEADS)]
    g = g_ref[...]


    cum = []
    for j in range(nbb):
        la = la_ref[j]
        p0 = la.astype(BF16)
        r0 = la - p0.astype(F32)
        p1 = r0.astype(BF16)
        p2 = (r0 - p1.astype(F32)).astype(BF16)
        bb = _dot(tril, jnp.concatenate([p0, p1, p2], axis=1))
        cum.append(bb[:, :GLA_QK] + bb[:, GLA_QK:2 * GLA_QK] + bb[:, 2 * GLA_QK:])

    o_inter, scores = [], []
    for j in range(nbb):
        b = cum[j]
        qk = qk_ref[j]
        q, k = qk[:, :GLA_QK], qk[:, GLA_QK:]
        v = v_ref[j]
        b0 = [jnp.zeros((1, GLA_QK), F32)] + [b[cb * i - 1:cb * i, :] for i in range(1, nsb)]
        btot = b[t - 1:t, :]
        bstart = jnp.concatenate([jnp.broadcast_to(b0[i], (cb, GLA_QK)) for i in range(nsb)], axis=0)
        q_rel = q * jnp.exp(b - bstart)
        q_int = (q * jnp.exp(b)).astype(BF16)
        k_end = k * jnp.exp(btot - b)

        s_old = s_scr[j]
        o_inter.append(_dot(q_int, s_old.astype(BF16)))
        ds = _dot(k_end.T.astype(BF16), v)
        dcol = jnp.exp(jnp.broadcast_to(btot, (V7X_SUBLANES, GLA_QK)).T[:, 0:1])
        s_scr[j] = s_old * dcol + ds * bd_ref[...]

        sc = []
        for i in range(nsb):
            n = cb * (i + 1)
            qs = q_rel[cb * i:cb * (i + 1)]
            q_stack = jnp.concatenate([qs * head_mask[h] for h in range(GLA_HEADS)], axis=0).astype(BF16)
            km = (k[:n] * jnp.exp(b0[i] - b[:n])).astype(BF16)
            sc.append(_dot_nt(q_stack, km))
        scores.append(sc)

    for j in range(nbb):
        v = v_ref[j]
        intra = []
        for i in range(nsb):
            n = cb * (i + 1)
            rr = lax.broadcasted_iota(I32, (GLA_HEADS * cb, n), 0)
            cc = lax.broadcasted_iota(I32, (GLA_HEADS * cb, n), 1)
            a = jnp.where(cc - cb * i <= (rr & (cb - 1)), scores[j][i], 0.0).astype(BF16)
            oi = _dot(a, v[:n])
            intra.append(jnp.concatenate(
                [oi[h * cb:(h + 1) * cb, h * GLA_DV:(h + 1) * GLA_DV] for h in range(GLA_HEADS)], axis=1))
        o = o_inter[j] + jnp.concatenate(intra, axis=0)
        on = jnp.concatenate([_rms(o[:, h * GLA_DV:(h + 1) * GLA_DV]) * g for h in range(GLA_HEADS)], axis=1)
        o_ref[j] = (on * sgr_ref[j].astype(F32)).astype(o_ref.dtype)

    @pl.when(c == pl.num_programs(1) - 1)
    def _():
        for j in range(nbb):
            for h in range(GLA_HEADS):
                st_ref[j, h] = s_scr[j, h * GLA_DK:(h + 1) * GLA_DK, h * GLA_DV:(h + 1) * GLA_DV]


def _gla(qk, v, la, sgr, s0, g_gla, bd_state, *, nbb=4):
    b, l, _ = qk.shape
    assert l % _GLA_T == 0
    while b % nbb:
        nbb -= 1
    tok = lambda w: pl.BlockSpec((nbb, _GLA_T, w), lambda i, c: (i, c, 0))
    st_spec = pl.BlockSpec((nbb, GLA_HEADS, GLA_DK, GLA_DV), lambda i, c: (i, 0, 0, 0))
    in_specs = [tok(2 * GLA_QK), tok(GLA_V), tok(GLA_QK), tok(GLA_V)]
    args = [qk, v, la, sgr]
    if s0 is not None:
        in_specs.append(st_spec)
        args.append(s0)
    in_specs += [_const_spec(g_gla.shape), _const_spec(bd_state.shape)]
    args += [g_gla, bd_state]
    return pl.pallas_call(
        functools.partial(_gla_kernel, has_s0=s0 is not None, nbb=nbb),
        out_shape=[jax.ShapeDtypeStruct((b, l, GLA_V), BF16),
                   jax.ShapeDtypeStruct((b, GLA_HEADS, GLA_DK, GLA_DV), F32)],
        grid=(b // nbb, l // _GLA_T),
        in_specs=in_specs,
        out_specs=[tok(GLA_V), st_spec],
        scratch_shapes=[pltpu.VMEM((nbb, GLA_QK, GLA_V), F32)],
        compiler_params=_params("parallel", "arbitrary"),
        name="gla",
    )(*args)


def _dsa_kernel(*refs, p_len, e_new, tq, topk, nbits):
    has_past = p_len > 0
    it = iter(refs)
    if has_past:
        kp_ref, vp_ref, ikp_ref = next(it), next(it), next(it)
    kn_ref, vtn_ref, ikn_ref, q_ref, iq_ref, iwt_ref = (next(it) for _ in range(6))
    o_ref, score_scr, aux_scr, bias_scr, x_scr = (next(it) for _ in range(5))
    e = p_len + e_new
    w0 = e - tq

    k_all, ik_all, vt = kn_ref[0], ikn_ref[0], vtn_ref[0]
    if has_past:
        k_all = jnp.concatenate([kp_ref[0], k_all], axis=0)
        ik_all = jnp.concatenate([ikp_ref[0], ik_all], axis=0)
        vt = jnp.concatenate([vp_ref[0].T.astype(BF16), vt], axis=1)

    def on_window(full, fn):
        return jnp.concatenate([full[:w0], fn(full[w0:])], axis=0) if w0 else fn(full)

    row = lax.broadcasted_iota(I32, (e, tq), 0)
    wr = lax.broadcasted_iota(I32, (tq, tq), 0)
    wc = lax.broadcasted_iota(I32, (tq, tq), 1)
    limit = (((w0 + wc) >> _CHUNK_SHIFT) + 1) << _CHUNK_SHIFT
    adm_win = (w0 + wr) < limit

    ik_hi = ik_all.astype(BF16)
    ik_lo = (ik_all - ik_hi.astype(F32)).astype(BF16)
    ik3 = jnp.concatenate([ik_hi, ik_lo, ik_hi], axis=1)
    iq = iq_ref[0]
    iwt = iwt_ref[0]
    score = jnp.zeros((e, tq), F32)
    for h in range(IDX_HEADS):
        iqh = iq[:, h * IDX_DIM:(h + 1) * IDX_DIM]
        hi = iqh.astype(BF16)
        lo = (iqh - hi.astype(F32)).astype(BF16)
        logit = _dot_nt(ik3, jnp.concatenate([hi, hi, lo], axis=1))
        score = score + iwt[h:h + 1, :] * jnp.maximum(logit, 0.0)
    score_scr[...] = on_window(score, lambda s: jnp.where(adm_win, s, -jnp.inf))

    def key_to_f32(c):
        return lax.bitcast_convert_type(c ^ ((c >> 31) & 0x7FFFFFFF), F32)

    def count_ge(cand):
        cf = key_to_f32(cand)
        slab = 8 * V7X_SUBLANES
        if e % slab:
            return _col_reduce(jnp.where(score_scr[...] >= cf, 1.0, 0.0), jnp.sum)
        acc = jnp.zeros((slab, tq), F32)
        for i in range(e // slab):
            acc = jnp.where(score_scr[i * slab:(i + 1) * slab, :] >= cf, acc + 1.0, acc)
        return jnp.sum(acc, axis=0, keepdims=True)

    c0 = count_ge(jnp.zeros((1, tq), I32))
    thr = jnp.where(c0 >= topk, 0, _INT_MIN).astype(I32)
    cnt = jnp.where(c0 >= topk, c0, float(e))

    def thr_step(i, carry):
        thr, cnt = carry
        cand = thr | lax.shift_left(jnp.int32(1), 30 - i)
        c = count_ge(cand)
        ok = c >= topk
        return jnp.where(ok, cand, thr), jnp.where(ok, c, cnt)

    thr, cnt = lax.fori_loop(0, 31, thr_step, (thr, cnt))
    thr_f = jnp.where(thr == _INT_MIN, -jnp.inf, key_to_f32(thr))

    x_scr[...] = jnp.full((1, tq), _POS_BIG, I32)

    @pl.when(jnp.max(cnt) > topk)
    def _():
        sc = score_scr[...]
        need = topk - _col_reduce(jnp.where(sc > thr_f, 1.0, 0.0), jnp.sum)
        aux_scr[...] = jnp.where(sc == thr_f, row, _POS_BIG)

        def tie_step(i, x):
            cand = x | lax.shift_left(jnp.int32(1), nbits - 1 - i)
            below = _col_reduce(jnp.where(aux_scr[...] < cand, 1.0, 0.0), jnp.sum)
            return jnp.where(below < need, cand, x)

        x_scr[...] = lax.fori_loop(0, nbits, tie_step, jnp.zeros((1, tq), I32))

    sc = score_scr[...]
    sel_bias = jnp.where(sc > thr_f, 0.0, jnp.where(sc == thr_f, jnp.where(row <= x_scr[...], 0.0, _NEG), _NEG))
    bias_scr[...] = on_window(sel_bias, lambda s: jnp.where(adm_win, s, _NEG))

    assert e <= _POS_SPLIT * _POS_SPLIT
    pr = lax.broadcasted_iota(I32, (e, ATT_HD), 0)
    pc = lax.broadcasted_iota(I32, (e, ATT_HD), 1)
    pos_cols = jnp.where(pc == 0, pr - (pr & (_POS_SPLIT - 1)), jnp.where(pc == 1, pr & (_POS_SPLIT - 1), 0))
    pos_cols = pos_cols.astype(F32).astype(BF16)
    k_aug = [jnp.concatenate([k_all[:, g * ATT_HD:(g + 1) * ATT_HD].astype(BF16), pos_cols], axis=1)
             for g in range(ATT_KV_HEADS)]
    ahead = -2.0 * jnp.maximum(wr - wc, 0).astype(F32)
    qc = lax.broadcasted_iota(I32, (tq, ATT_HD), 1)
    q = q_ref[0]

    def qk_logits(h):
        slope = 2.0 ** (-8.0 * (h + 1) / ATT_HEADS)
        slope_cols = jnp.where(qc < 2, slope, 0.0).astype(BF16)
        q_aug = jnp.concatenate([q[:, h * ATT_HD:(h + 1) * ATT_HD], slope_cols], axis=1)
        return _dot_nt(k_aug[h // ATT_REP], q_aug)

    outs = []
    s_next = qk_logits(0)
    for h in range(ATT_HEADS):
        g = h // ATT_REP
        slope = 2.0 ** (-8.0 * (h + 1) / ATT_HEADS)
        s = s_next
        if h + 1 < ATT_HEADS:
            s_next = qk_logits(h + 1)
        s = on_window(s, lambda sw: sw + slope * ahead) + bias_scr[...]
        m = _col_reduce(s, jnp.max)
        p = jnp.exp(s - m)
        denom = _col_reduce(p, jnp.sum)
        ot = _dot(vt[g * ATT_HD:(g + 1) * ATT_HD, :], p.astype(BF16))
        outs.append(ot / denom)
    ot = jnp.concatenate(outs, axis=0)
    if tq % V7X_LANES:
        ot = jnp.concatenate([ot, jnp.zeros((ATT_Q, V7X_LANES - tq % V7X_LANES), F32)], axis=1)
    o_ref[0] = ot.T[:tq].astype(o_ref.dtype)


def _dsa_call(q_att, iq, iwt, k_new, vt_new, ik_new, past, *, tq, tile, topk):
    b = q_att.shape[0]
    p_len = 0 if past is None else past[0].shape[1]
    e_new = (tile + 1) * tq
    e = p_len + e_new
    in_specs, args = [], []
    if past is not None:
        in_specs += [pl.BlockSpec((1, p_len, w), lambda i: (i, 0, 0)) for w in (ATT_KV, ATT_KV, IDX_DIM)]
        args += list(past)
    in_specs += [
        pl.BlockSpec((1, e_new, ATT_KV), lambda i: (i, 0, 0)),
        pl.BlockSpec((1, ATT_KV, e_new), lambda i: (i, 0, 0)),
        pl.BlockSpec((1, e_new, IDX_DIM), lambda i: (i, 0, 0)),
        pl.BlockSpec((1, tq, ATT_Q), lambda i: (i, tile, 0)),
        pl.BlockSpec((1, tq, IDX_Q), lambda i: (i, tile, 0)),
        pl.BlockSpec((1, _IWT_ROWS, tq), lambda i: (i, 0, tile)),
    ]
    args += [k_new, vt_new, ik_new, q_att, iq, iwt]
    kern = functools.partial(_dsa_kernel, p_len=p_len, e_new=e_new, tq=tq, topk=topk,
                             nbits=max(1, (e - 1).bit_length()))
    return pl.pallas_call(
        kern,
        out_shape=jax.ShapeDtypeStruct((b, tq, ATT_Q), BF16),
        grid=(b,),
        in_specs=in_specs,
        out_specs=pl.BlockSpec((1, tq, ATT_Q), lambda i: (i, 0, 0)),
        scratch_shapes=[
            pltpu.VMEM((e, tq), F32),
            pltpu.VMEM((e, tq), I32),
            pltpu.VMEM((e, tq), F32),
            pltpu.VMEM((1, tq), I32),
        ],
        compiler_params=_params("parallel"),
        name=f"dsa_e{e}",
    )(*args)


def _dsa(q_att, iq, iwt, k_new, vt_new, ik_new, past):
    b, l, _ = q_att.shape
    p_len = 0 if past is None else past[0].shape[1]
    topk = min(TOPK_MAX, (p_len + l) // 4)
    if l <= CHUNK:
        return _dsa_call(q_att, iq, iwt, k_new, vt_new, ik_new, past, tq=l, tile=0, topk=topk)
    assert l % _DSA_TQ == 0
    outs = [_dsa_call(q_att, iq, iwt, k_new, vt_new, ik_new, past, tq=_DSA_TQ, tile=t, topk=topk)
            for t in range(l // _DSA_TQ)]
    return jnp.concatenate(outs, axis=1)


def _block_diag_ones(n, blk):
    i = jnp.arange(n) // blk
    return (i[:, None] == i[None, :])


def _prep_weights(w_ada, b_ada, g_ffn1, w1_gate, w1_up, w1_down, g_mix, w_in, w_a2, b_a, g_gla,
                  g_q, g_k, w_out, g_ffn2, w2_gate, w2_up, w2_down, g_out):
    sizes = (GLA_QK, GLA_QK, GLA_V, GLA_LOWRANK, GLA_V, ATT_Q, ATT_KV, ATT_KV, IDX_Q, IDX_DIM, IDX_HEADS)
    pts, acc = [], 0
    for s in sizes[:-1]:
        acc += s
        pts.append(acc)
    gq, gk, gv, ga, gr, aq, ak, av, iq, ik, iw = jnp.split(w_in, pts, axis=1)
    pad = jnp.zeros((D_MODEL, _PACKED_WIDTH - sum(sizes)), w_in.dtype)
    w_packed = jnp.concatenate([gq, gk, gv, gr, aq, ak, av, iq, ik, ga, iw, pad], axis=1).astype(BF16)
    wa2 = jnp.zeros((V7X_LANES, GLA_QK), F32).at[_MISC_GA:_MISC_GA + GLA_LOWRANK].set(w_a2).astype(BF16)
    row = lambda a: a.reshape(1, -1).astype(F32)
    bd_state = jnp.repeat(jnp.repeat(jnp.eye(GLA_HEADS, dtype=F32), GLA_DK, axis=0), GLA_DV, axis=1)
    return dict(
        w_ada=w_ada.astype(BF16), b_ada=row(b_ada),
        g_ffn1=row(g_ffn1), w1=(w1_gate.astype(BF16), w1_up.astype(BF16), w1_down.astype(BF16)),
        g_mix=row(g_mix), w_packed=w_packed, wa2=wa2, b_a=row(b_a),
        g_gla=row(g_gla), gq_t=row(jnp.tile(g_q, ATT_HEADS)), gk_t=row(jnp.tile(g_k, ATT_KV_HEADS)),
        bd_heads=_block_diag_ones(ATT_Q, ATT_HD).astype(BF16), bd_state=bd_state,
        w_out=w_out.astype(BF16),
        g_ffn2=row(g_ffn2), w2=(w2_gate.astype(BF16), w2_up.astype(BF16), w2_down.astype(BF16)),
        g_out=row(g_out),
    )


def _layer(x, mod, past, s0, w):
    b, l, _ = x.shape
    x1 = _ffn(x, mod, w["g_ffn1"], *w["w1"], mod_base=0)
    (qk, gv, sgr, la, q_att, k_new, v_new, vt_new, ik_new, iq, iwt) = _proj(
        x1, mod, w["g_mix"], w["w_packed"], w["wa2"], w["b_a"], w["gq_t"], w["gk_t"], w["bd_heads"])
    gla_out, s_t = _gla(qk, gv, la, sgr, s0, w["g_gla"], w["bd_state"])
    att = _dsa(q_att, iq, iwt, k_new, vt_new, ik_new, past)
    y = _ffn(x1, mod, w["g_ffn2"], *w["w2"], mod_base=6, mix=(gla_out, att, w["w_out"]), g_out=w["g_out"])
    kv_shape = (b, l, ATT_KV_HEADS, ATT_HD)
    return y, k_new.reshape(kv_shape), v_new.reshape(kv_shape), ik_new, s_t


def kernel(x_prompt, x_sample, c_prompt, c_sample, cache_k, cache_v, cache_idx_k, state_gla, w_ada, b_ada, g_ffn1, w1_gate, w1_up, w1_down, g_mix, w_in, w_a2, b_a, g_gla, g_q, g_k, w_out, g_ffn2, w2_gate, w2_up, w2_down, g_out):
    depth = w_ada.shape[0]
    bp = x_prompt.shape[0]
    yp, ys = x_prompt, x_sample
    outs_p, outs_s = [], []
    for layer in range(depth):
        w = _prep_weights(*(t[layer] for t in (
            w_ada, b_ada, g_ffn1, w1_gate, w1_up, w1_down, g_mix, w_in, w_a2, b_a, g_gla,
            g_q, g_k, w_out, g_ffn2, w2_gate, w2_up, w2_down, g_out)))
        mod = _adaln(jnp.concatenate([c_prompt, c_sample], axis=0), w["w_ada"], w["b_ada"])
        mod = mod.reshape(mod.shape[0], 9, D_MODEL)
        ds, pp = cache_k.shape[1], cache_k.shape[2]
        past = (cache_k[layer].reshape(ds, pp, ATT_KV), cache_v[layer].reshape(ds, pp, ATT_KV), cache_idx_k[layer])
        yp, *rest_p = _layer(yp, mod[:bp], None, None, w)
        ys, *rest_s = _layer(ys, mod[bp:], past, state_gla[layer], w)
        outs_p.append(rest_p)
        outs_s.append(rest_s)
    stack = lambda outs, i: jnp.stack([o[i] for o in outs])
    return (yp, ys,
            stack(outs_p, 0), stack(outs_p, 1), stack(outs_p, 2), stack(outs_p, 3),
            stack(outs_s, 0), stack(outs_s, 1), stack(outs_s, 2), stack(outs_s, 3))
```

```python
import functools

import jax
import jax.numpy as jnp
from jax import lax
from jax.experimental import pallas as pl
from jax.experimental.pallas import tpu as pltpu

F32 = jnp.float32
BF16 = jnp.bfloat16
I32 = jnp.int32

D_MODEL = 1024
D_FF = 2816
CHUNK = 64
GLA_HEADS = 4
GLA_DK = 64
GLA_DV = 128
GLA_LOWRANK = 16
GLA_TAU = 16.0
GLA_BLOCK = 16
ATT_HEADS = 8
ATT_KV_HEADS = 2
ATT_HD = 64
IDX_HEADS = 4
IDX_DIM = 64
TOPK_MAX = 256
EPS = 1e-6

GLA_QK = GLA_HEADS * GLA_DK
GLA_V = GLA_HEADS * GLA_DV
ATT_Q = ATT_HEADS * ATT_HD
ATT_KV = ATT_KV_HEADS * ATT_HD
IDX_Q = IDX_HEADS * IDX_DIM
MIX_WIDTH = GLA_V + ATT_Q
ATT_REP = ATT_HEADS // ATT_KV_HEADS

V7X_LANES = 128
V7X_SUBLANES = 8
V7X_VMEM_LIMIT_BYTES = 56 * 1024 * 1024

_SEG_GQK = 0
_SEG_GV = 512
_SEG_GR = 1024
_SEG_AQ = 1536
_SEG_AKV = 2048
_SEG_IQ = 2304
_SEG_MISC = 2560
_PACKED_WIDTH = 2688
_MISC_GA = IDX_DIM
_MISC_IW = IDX_DIM + GLA_LOWRANK
_IWT_ROWS = V7X_SUBLANES

_FF_CHUNK = 256
_GLA_T = 64
_DSA_TQ = 256
_DSA_LOGIT_ROWS_IN_FLIGHT = 4096
_NEG = -1e30
_INT_MIN = -(2 ** 31)
_POS_BIG = 2 ** 30
_CHUNK_SHIFT = CHUNK.bit_length() - 1
_POS_SPLIT = 256
assert 1 << _CHUNK_SHIFT == CHUNK

_NT = (((1,), (1,)), ((), ()))


def _dot(a, b):
    return jnp.dot(a, b, preferred_element_type=F32)


def _dot_nt(a, b):
    return lax.dot_general(a, b, _NT, preferred_element_type=F32)


def _rms(x):
    return x * lax.rsqrt(jnp.mean(x * x, axis=-1, keepdims=True) + EPS)


def _silu(x):
    return x * jax.nn.sigmoid(x)


def _col_reduce(x, op, chains=8):
    r, n = x.shape
    while chains > 1 and r % (chains * V7X_SUBLANES):
        chains //= 2
    if r % (chains * V7X_SUBLANES):
        return op(x, axis=0, keepdims=True)
    slab = chains * V7X_SUBLANES
    pair = jnp.add if op is jnp.sum else jnp.maximum
    acc = x[:slab]
    for i in range(1, r // slab):
        acc = pair(acc, x[i * slab:(i + 1) * slab])
    return op(acc, axis=0, keepdims=True)


def _row_tiling(batch, length, target):
    if length >= target:
        assert length % target == 0
        return 1, target
    nb = max(1, min(batch, target // length))
    while batch % nb:
        nb -= 1
    return nb, length


def _const_spec(shape):
    zeros = (0,) * len(shape)
    return pl.BlockSpec(shape, lambda *_: zeros, pipeline_mode=pl.Buffered(1))


def _params(*sem):
    return pltpu.CompilerParams(dimension_semantics=sem, vmem_limit_bytes=V7X_VMEM_LIMIT_BYTES)


def _adaln_kernel(c_ref, w_ref, b_ref, o_ref):
    a = _silu(c_ref[...]).astype(BF16)
    o_ref[...] = _dot(a, w_ref[...]) + b_ref[...]


def _adaln(c, w_ada, b_ada):
    bt, d = c.shape
    n = w_ada.shape[1]
    tn = d
    return pl.pallas_call(
        _adaln_kernel,
        out_shape=jax.ShapeDtypeStruct((bt, n), F32),
        grid=(n // tn,),
        in_specs=[
            pl.BlockSpec((bt, d), lambda j: (0, 0)),
            pl.BlockSpec((d, tn), lambda j: (0, j)),
            pl.BlockSpec((1, tn), lambda j: (0, j)),
        ],
        out_specs=pl.BlockSpec((bt, tn), lambda j: (0, j)),
        compiler_params=_params("arbitrary"),
        name="adaln",
    )(c, w_ada, b_ada)


def _ffn_kernel(*refs, nb, rows, mod_base, with_mix, with_final_norm):
    it = iter(refs)
    x_ref, mod_ref = next(it), next(it)
    if with_mix:
        gla_ref, att_ref, wout_ref = next(it), next(it), next(it)
    g_ref, wg_ref, wu_ref, wd_ref = next(it), next(it), next(it), next(it)
    if with_final_norm:
        gout_ref = next(it)
    o_ref, h_scr, x_scr = next(it), next(it), next(it)
    tm = nb * rows

    x = x_ref[...].reshape(tm, D_MODEL)
    if with_mix:
        gla = gla_ref[...].reshape(tm, GLA_V)
        att = att_ref[...].reshape(tm, ATT_Q)
        mix = _dot(gla, wout_ref[:GLA_V, :]) + _dot(att, wout_ref[GLA_V:, :])
    g = g_ref[...]
    for j in range(nb):
        sl = slice(j * rows, (j + 1) * rows)
        m = mod_ref[j]
        xj = x[sl]
        if with_mix:
            xj = xj + m[5:6] * mix[sl]
        x_scr[sl, :] = xj
        h = (_rms(xj) * g) * (1.0 + m[mod_base + 1:mod_base + 2]) + m[mod_base:mod_base + 1]
        h_scr[sl, :] = h.astype(BF16)

    h = h_scr[...]
    acc = jnp.zeros((tm, D_MODEL), F32)
    for c in range(D_FF // _FF_CHUNK):
        cs = slice(c * _FF_CHUNK, (c + 1) * _FF_CHUNK)
        gate = _dot(h, wg_ref[:, cs])
        up = _dot(h, wu_ref[:, cs])
        a = (_silu(gate) * up).astype(BF16)
        acc = acc + _dot(a, wd_ref[cs, :])

    for j in range(nb):
        sl = slice(j * rows, (j + 1) * rows)
        m = mod_ref[j]
        y = x_scr[sl, :] + 0.5 * m[mod_base + 2:mod_base + 3] * acc[sl]
        if with_final_norm:
            y = _rms(y) * gout_ref[...]
        o_ref[j] = y


def _ffn(x, mod, g, wg, wu, wd, *, mod_base, mix=None, g_out=None, tm=512):
    b, l, d = x.shape
    nb, rows = _row_tiling(b, l, tm)
    grid = (b // nb, l // rows)
    row_spec = lambda w: pl.BlockSpec((nb, rows, w), lambda i, r: (i, r, 0))
    in_specs = [row_spec(d), pl.BlockSpec((nb, 9, d), lambda i, r: (i, 0, 0))]
    args = [x, mod]
    if mix is not None:
        gla, att, wout = mix
        in_specs += [row_spec(GLA_V), row_spec(ATT_Q), _const_spec(wout.shape)]
        args += [gla, att, wout]
    in_specs += [_const_spec(g.shape), _const_spec(wg.shape), _const_spec(wu.shape), _const_spec(wd.shape)]
    args += [g, wg, wu, wd]
    if g_out is not None:
        in_specs.append(_const_spec(g_out.shape))
        args.append(g_out)
    kern = functools.partial(
        _ffn_kernel, nb=nb, rows=rows, mod_base=mod_base,
        with_mix=mix is not None, with_final_norm=g_out is not None)
    return pl.pallas_call(
        kern,
        out_shape=jax.ShapeDtypeStruct((b, l, d), F32),
        grid=grid,
        in_specs=in_specs,
        out_specs=row_spec(d),
        scratch_shapes=[pltpu.VMEM((nb * rows, d), BF16), pltpu.VMEM((nb * rows, d), F32)],
        compiler_params=_params("parallel", "parallel"),
        name="ffn_mix" if mix is not None else "ffn",
    )(*args)


def _proj_kernel(x_ref, mod_ref, g_ref, w_ref, wa2_ref, ba_ref, gq_ref, gk_ref, bd_ref,
                 qk_ref, v_ref, sgr_ref, la_ref, qatt_ref, knew_ref, vnew_ref, vtnew_ref,
                 iknew_ref, iq_ref, iwt_ref, h_scr, *, nb, rows):
    tm = nb * rows
    x = x_ref[...].reshape(tm, D_MODEL)
    g = g_ref[...]
    for j in range(nb):
        sl = slice(j * rows, (j + 1) * rows)
        m = mod_ref[j]
        h = (_rms(x[sl]) * g) * (1.0 + m[4:5]) + m[3:4]
        h_scr[sl, :] = h.astype(BF16)
    h = h_scr[...]

    def seg(start, width):
        return _dot(h, w_ref[:, start:start + width])

    def put(ref, val, width):
        ref[...] = val.reshape(nb, rows, width).astype(ref.dtype)

    lane = lax.broadcasted_iota(I32, (1, 2 * GLA_QK), 1)
    qscale = jnp.where(lane < GLA_QK, GLA_DK ** -0.5, 1.0).astype(F32)
    put(qk_ref, seg(_SEG_GQK, 2 * GLA_QK) * qscale, 2 * GLA_QK)
    put(v_ref, seg(_SEG_GV, GLA_V), GLA_V)
    put(sgr_ref, _silu(seg(_SEG_GR, GLA_V)), GLA_V)

    aq = seg(_SEG_AQ, ATT_Q)
    msq = _dot((aq * aq).astype(BF16), bd_ref[...]) * (1.0 / ATT_HD)
    put(qatt_ref, aq * lax.rsqrt(msq + EPS) * gq_ref[...] * (ATT_HD ** -0.5), ATT_Q)

    akv = seg(_SEG_AKV, 2 * ATT_KV)
    ak, av = akv[:, :ATT_KV], akv[:, ATT_KV:]
    msk = _dot((ak * ak).astype(BF16), bd_ref[:ATT_KV, :ATT_KV]) * (1.0 / ATT_HD)
    put(knew_ref, ak * lax.rsqrt(msk + EPS) * gk_ref[...], ATT_KV)
    put(vnew_ref, av, ATT_KV)
    avt = av.T
    for j in range(nb):
        vtnew_ref[j] = avt[:, j * rows:(j + 1) * rows].astype(BF16)

    put(iq_ref, seg(_SEG_IQ, IDX_Q), IDX_Q)

    misc = seg(_SEG_MISC, V7X_LANES)
    put(iknew_ref, misc[:, :IDX_DIM], IDX_DIM)
    za = _dot(misc.astype(BF16), wa2_ref[...]) + ba_ref[...]
    log_sig = jnp.minimum(za, 0.0) - jnp.log(1.0 + jnp.exp(-jnp.abs(za)))
    put(la_ref, log_sig * (1.0 / GLA_TAU), GLA_QK)
    misct = misc.T
    iwt = misct[_MISC_IW:_MISC_IW + _IWT_ROWS, :] * ((IDX_HEADS * IDX_DIM) ** -0.5)
    for j in range(nb):
        iwt_ref[j] = iwt[:, j * rows:(j + 1) * rows]


def _proj(x, mod, g_mix, w_packed, wa2, b_a, gq_t, gk_t, bd, *, tm=512):
    b, l, d = x.shape
    nb, rows = _row_tiling(b, l, tm)
    grid = (b // nb, l // rows)
    row_spec = lambda w: pl.BlockSpec((nb, rows, w), lambda i, r: (i, r, 0))
    col_spec = lambda h: pl.BlockSpec((nb, h, rows), lambda i, r: (i, 0, r))
    sds = lambda w, dt: jax.ShapeDtypeStruct((b, l, w), dt)
    out_shape = [
        sds(2 * GLA_QK, F32), sds(GLA_V, BF16), sds(GLA_V, BF16), sds(GLA_QK, F32), sds(ATT_Q, BF16),
        sds(ATT_KV, F32), sds(ATT_KV, F32), jax.ShapeDtypeStruct((b, ATT_KV, l), BF16),
        sds(IDX_DIM, F32), sds(IDX_Q, F32), jax.ShapeDtypeStruct((b, _IWT_ROWS, l), F32),
    ]
    out_specs = [
        row_spec(2 * GLA_QK), row_spec(GLA_V), row_spec(GLA_V), row_spec(GLA_QK), row_spec(ATT_Q),
        row_spec(ATT_KV), row_spec(ATT_KV), col_spec(ATT_KV),
        row_spec(IDX_DIM), row_spec(IDX_Q), col_spec(_IWT_ROWS),
    ]
    consts = [g_mix, w_packed, wa2, b_a, gq_t, gk_t, bd]
    return pl.pallas_call(
        functools.partial(_proj_kernel, nb=nb, rows=rows),
        out_shape=out_shape,
        grid=grid,
        in_specs=[row_spec(d), pl.BlockSpec((nb, 9, d), lambda i, r: (i, 0, 0))]
        + [_const_spec(c.shape) for c in consts],
        out_specs=out_specs,
        scratch_shapes=[pltpu.VMEM((nb * rows, d), BF16)],
        compiler_params=_params("parallel", "parallel"),
        name="proj",
    )(x, mod, *consts)


def _gla_kernel(*refs, has_s0, nbb):
    it = iter(refs)
    qk_ref, v_ref, la_ref, sgr_ref = next(it), next(it), next(it), next(it)
    s0_ref = next(it) if has_s0 else None
    g_ref, bd_ref, o_ref, st_ref, s_scr = next(it), next(it), next(it), next(it), next(it)
    t, cb = _GLA_T, GLA_BLOCK
    nsb = t // cb
    c = pl.program_id(1)

    @pl.when(c == 0)
    def _():
        s_scr[...] = jnp.zeros((nbb, GLA_QK, GLA_V), F32)
        if has_s0:
            for j in range(nbb):
                for h in range(GLA_HEADS):
                    s_scr[j, h * GLA_DK:(h + 1) * GLA_DK, h * GLA_DV:(h + 1) * GLA_DV] = s0_ref[j, h]

    ri = lax.broadcasted_iota(I32, (t, t), 0)
    ci = lax.broadcasted_iota(I32, (t, t), 1)
    tril = jnp.where(ri >= ci, 1.0, 0.0).astype(BF16)
    lane = lax.broadcasted_iota(I32, (1, GLA_QK), 1)
    head_mask = [jnp.where((lane >= h * GLA_DK) & (lane < (h + 1) * GLA_DK), 1.0, 0.0).astype(F32)
                 for h in range(GLA_HEADS)]
    g = g_ref[...]


    cum = []
    for j in range(nbb):
        la = la_ref[j]
        p0 = la.astype(BF16)
        r0 = la - p0.astype(F32)
        p1 = r0.astype(BF16)
        p2 = (r0 - p1.astype(F32)).astype(BF16)
        bb = _dot(tril, jnp.concatenate([p0, p1, p2], axis=1))
        cum.append(bb[:, :GLA_QK] + bb[:, GLA_QK:2 * GLA_QK] + bb[:, 2 * GLA_QK:])

    o_inter, scores = [], []
    for j in range(nbb):
        b = cum[j]
        qk = qk_ref[j]
        q, k = qk[:, :GLA_QK], qk[:, GLA_QK:]
        v = v_ref[j]
        b0 = [jnp.zeros((1, GLA_QK), F32)] + [b[cb * i - 1:cb * i, :] for i in range(1, nsb)]
        btot = b[t - 1:t, :]
        bstart = jnp.concatenate([jnp.broadcast_to(b0[i], (cb, GLA_QK)) for i in range(nsb)], axis=0)
        q_rel = q * jnp.exp(b - bstart)
        q_int = (q * jnp.exp(b)).astype(BF16)
        k_end = k * jnp.exp(btot - b)

        s_old = s_scr[j]
        o_inter.append(_dot(q_int, s_old.astype(BF16)))
        ds = _dot(k_end.T.astype(BF16), v)
        dcol = jnp.exp(jnp.broadcast_to(btot, (V7X_SUBLANES, GLA_QK)).T[:, 0:1])
        s_scr[j] = s_old * dcol + ds * bd_ref[...]

        sc = []
        for i in range(nsb):
            n = cb * (i + 1)
            qs = q_rel[cb * i:cb * (i + 1)]
            q_stack = jnp.concatenate([qs * head_mask[h] for h in range(GLA_HEADS)], axis=0).astype(BF16)
            km = (k[:n] * jnp.exp(b0[i] - b[:n])).astype(BF16)
            sc.append(_dot_nt(q_stack, km))
        scores.append(sc)

    for j in range(nbb):
        v = v_ref[j]
        intra = []
        for i in range(nsb):
            n = cb * (i + 1)
            rr = lax.broadcasted_iota(I32, (GLA_HEADS * cb, n), 0)
            cc = lax.broadcasted_iota(I32, (GLA_HEADS * cb, n), 1)
            a = jnp.where(cc - cb * i <= (rr & (cb - 1)), scores[j][i], 0.0).astype(BF16)
            oi = _dot(a, v[:n])
            intra.append(jnp.concatenate(
                [oi[h * cb:(h + 1) * cb, h * GLA_DV:(h + 1) * GLA_DV] for h in range(GLA_HEADS)], axis=1))
        o = o_inter[j] + jnp.concatenate(intra, axis=0)
        on = jnp.concatenate([_rms(o[:, h * GLA_DV:(h + 1) * GLA_DV]) * g for h in range(GLA_HEADS)], axis=1)
        o_ref[j] = (on * sgr_ref[j].astype(F32)).astype(o_ref.dtype)

    @pl.when(c == pl.num_programs(1) - 1)
    def _():
        for j in range(nbb):
            for h in range(GLA_HEADS):
                st_ref[j, h] = s_scr[j, h * GLA_DK:(h + 1) * GLA_DK, h * GLA_DV:(h + 1) * GLA_DV]


def _gla(qk, v, la, sgr, s0, g_gla, bd_state, *, nbb=4):
    b, l, _ = qk.shape
    assert l % _GLA_T == 0
    while b % nbb:
        nbb -= 1
    tok = lambda w: pl.BlockSpec((nbb, _GLA_T, w), lambda i, c: (i, c, 0))
    st_spec = pl.BlockSpec((nbb, GLA_HEADS, GLA_DK, GLA_DV), lambda i, c: (i, 0, 0, 0))
    in_specs = [tok(2 * GLA_QK), tok(GLA_V), tok(GLA_QK), tok(GLA_V)]
    args = [qk, v, la, sgr]
    if s0 is not None:
        in_specs.append(st_spec)
        args.append(s0)
    in_specs += [_const_spec(g_gla.shape), _const_spec(bd_state.shape)]
    args += [g_gla, bd_state]
    return pl.pallas_call(
        functools.partial(_gla_kernel, has_s0=s0 is not None, nbb=nbb),
        out_shape=[jax.ShapeDtypeStruct((b, l, GLA_V), BF16),
                   jax.ShapeDtypeStruct((b, GLA_HEADS, GLA_DK, GLA_DV), F32)],
        grid=(b // nbb, l // _GLA_T),
        in_specs=in_specs,
        out_specs=[tok(GLA_V), st_spec],
        scratch_shapes=[pltpu.VMEM((nbb, GLA_QK, GLA_V), F32)],
        compiler_params=_params("parallel", "arbitrary"),
        name="gla",
    )(*args)


def _dsa_kernel(*refs, p_len, e_new, tq, topk, nbits):
    has_past = p_len > 0
    it = iter(refs)
    if has_past:
        kp_ref, vp_ref, ikp_ref = next(it), next(it), next(it)
    kn_ref, vtn_ref, ikn_ref, q_ref, iq_ref, iwt_ref = (next(it) for _ in range(6))
    o_ref, score_scr, aux_scr, bias_scr, x_scr = (next(it) for _ in range(5))
    e = p_len + e_new
    w0 = e - tq

    k_all, ik_all, vt = kn_ref[0], ikn_ref[0], vtn_ref[0]
    if has_past:
        k_all = jnp.concatenate([kp_ref[0], k_all], axis=0)
        ik_all = jnp.concatenate([ikp_ref[0], ik_all], axis=0)
        vt = jnp.concatenate([vp_ref[0].T.astype(BF16), vt], axis=1)

    def on_window(full, fn):
        return jnp.concatenate([full[:w0], fn(full[w0:])], axis=0) if w0 else fn(full)

    row = lax.broadcasted_iota(I32, (e, tq), 0)
    wr = lax.broadcasted_iota(I32, (tq, tq), 0)
    wc = lax.broadcasted_iota(I32, (tq, tq), 1)
    limit = (((w0 + wc) >> _CHUNK_SHIFT) + 1) << _CHUNK_SHIFT
    adm_win = (w0 + wr) < limit

    ik_hi = ik_all.astype(BF16)
    ik_lo = (ik_all - ik_hi.astype(F32)).astype(BF16)
    ik3 = jnp.concatenate([ik_hi, ik_lo, ik_hi], axis=1)
    iq = iq_ref[0]
    iwt = iwt_ref[0]
    wide = tq % V7X_LANES == 0

    def iq_pieces(h):
        iqh = iq[:, h * IDX_DIM:(h + 1) * IDX_DIM]
        hi = iqh.astype(BF16)
        lo = (iqh - hi.astype(F32)).astype(BF16)
        return jnp.concatenate([hi, hi, lo], axis=1)

    if wide:
        logits = _dot_nt(ik3, jnp.concatenate([iq_pieces(h) for h in range(IDX_HEADS)], axis=0))
        logit = [logits[:, h * tq:(h + 1) * tq] for h in range(IDX_HEADS)]
    else:
        logit = [_dot_nt(ik3, iq_pieces(h)) for h in range(IDX_HEADS)]
    score = jnp.zeros((e, tq), F32)
    for h in range(IDX_HEADS):
        score = score + iwt[h:h + 1, :] * jnp.maximum(logit[h], 0.0)
    score_scr[...] = on_window(score, lambda s: jnp.where(adm_win, s, -jnp.inf))

    def key_to_f32(c):
        return lax.bitcast_convert_type(c ^ ((c >> 31) & 0x7FFFFFFF), F32)

    def count_ge(cand):
        cf = key_to_f32(cand)
        slab = 8 * V7X_SUBLANES
        if e % slab:
            return _col_reduce(jnp.where(score_scr[...] >= cf, 1.0, 0.0), jnp.sum)
        acc = jnp.zeros((slab, tq), F32)
        for i in range(e // slab):
            acc = jnp.where(score_scr[i * slab:(i + 1) * slab, :] >= cf, acc + 1.0, acc)
        return jnp.sum(acc, axis=0, keepdims=True)

    c0 = count_ge(jnp.zeros((1, tq), I32))
    thr = jnp.where(c0 >= topk, 0, _INT_MIN).astype(I32)
    cnt = jnp.where(c0 >= topk, c0, float(e))

    def thr_step(i, carry):
        thr, cnt = carry
        cand = thr | lax.shift_left(jnp.int32(1), 30 - i)
        c = count_ge(cand)
        ok = c >= topk
        return jnp.where(ok, cand, thr), jnp.where(ok, c, cnt)

    thr, cnt = lax.fori_loop(0, 31, thr_step, (thr, cnt))
    thr_f = jnp.where(thr == _INT_MIN, -jnp.inf, key_to_f32(thr))

    x_scr[...] = jnp.full((1, tq), _POS_BIG, I32)

    @pl.when(jnp.max(cnt) > topk)
    def _():
        sc = score_scr[...]
        need = topk - _col_reduce(jnp.where(sc > thr_f, 1.0, 0.0), jnp.sum)
        aux_scr[...] = jnp.where(sc == thr_f, row, _POS_BIG)

        def tie_step(i, x):
            cand = x | lax.shift_left(jnp.int32(1), nbits - 1 - i)
            below = _col_reduce(jnp.where(aux_scr[...] < cand, 1.0, 0.0), jnp.sum)
            return jnp.where(below < need, cand, x)

        x_scr[...] = lax.fori_loop(0, nbits, tie_step, jnp.zeros((1, tq), I32))

    sc = score_scr[...]
    sel_bias = jnp.where(sc > thr_f, 0.0, jnp.where(sc == thr_f, jnp.where(row <= x_scr[...], 0.0, _NEG), _NEG))
    bias_scr[...] = on_window(sel_bias, lambda s: jnp.where(adm_win, s, _NEG))

    assert e <= _POS_SPLIT * _POS_SPLIT
    pr = lax.broadcasted_iota(I32, (e, ATT_HD), 0)
    pc = lax.broadcasted_iota(I32, (e, ATT_HD), 1)
    pos_cols = jnp.where(pc == 0, pr - (pr & (_POS_SPLIT - 1)), jnp.where(pc == 1, pr & (_POS_SPLIT - 1), 0))
    pos_cols = pos_cols.astype(F32).astype(BF16)
    k_aug = [jnp.concatenate([k_all[:, g * ATT_HD:(g + 1) * ATT_HD].astype(BF16), pos_cols], axis=1)
             for g in range(ATT_KV_HEADS)]
    ahead = -2.0 * jnp.maximum(wr - wc, 0).astype(F32)
    qc = lax.broadcasted_iota(I32, (tq, ATT_HD), 1)
    q = q_ref[0]

    def slope_of(h):
        return 2.0 ** (-8.0 * (h + 1) / ATT_HEADS)

    def q_aug(h):
        slope_cols = jnp.where(qc < 2, slope_of(h), 0.0).astype(BF16)
        return jnp.concatenate([q[:, h * ATT_HD:(h + 1) * ATT_HD], slope_cols], axis=1)

    def softmax_cols(s, h):
        s = on_window(s, lambda sw: sw + slope_of(h) * ahead) + bias_scr[...]
        p = jnp.exp(s - _col_reduce(s, jnp.max))
        return p.astype(BF16), _col_reduce(p, jnp.sum)

    outs = []
    if wide:
        group_logits = [
            _dot_nt(k_aug[g], jnp.concatenate([q_aug(g * ATT_REP + r) for r in range(ATT_REP)], axis=0))
            for g in range(ATT_KV_HEADS)]
        for g in range(ATT_KV_HEADS):
            probs, denoms = [], []
            for r in range(ATT_REP):
                p, denom = softmax_cols(group_logits[g][:, r * tq:(r + 1) * tq], g * ATT_REP + r)
                probs.append(p)
                denoms.append(denom)
            ot = _dot(vt[g * ATT_HD:(g + 1) * ATT_HD, :], jnp.concatenate(probs, axis=1))
            outs += [ot[:, r * tq:(r + 1) * tq] / denoms[r] for r in range(ATT_REP)]
    else:
        ahead_heads = max(1, min(ATT_HEADS - 1, _DSA_LOGIT_ROWS_IN_FLIGHT // e))
        qk = lambda h: _dot_nt(k_aug[h // ATT_REP], q_aug(h))
        pending = [qk(h) for h in range(ahead_heads)]
        for h in range(ATT_HEADS):
            s = pending.pop(0)
            if h + ahead_heads < ATT_HEADS:
                pending.append(qk(h + ahead_heads))
            p, denom = softmax_cols(s, h)
            g = h // ATT_REP
            outs.append(_dot(vt[g * ATT_HD:(g + 1) * ATT_HD, :], p) / denom)
    ot = jnp.concatenate(outs, axis=0)
    if tq % V7X_LANES:
        ot = jnp.concatenate([ot, jnp.zeros((ATT_Q, V7X_LANES - tq % V7X_LANES), F32)], axis=1)
    o_ref[0] = ot.T[:tq].astype(o_ref.dtype)


def _dsa_call(q_att, iq, iwt, k_new, vt_new, ik_new, past, *, tq, tile, topk):
    b = q_att.shape[0]
    p_len = 0 if past is None else past[0].shape[1]
    e_new = (tile + 1) * tq
    e = p_len + e_new
    in_specs, args = [], []
    if past is not None:
        in_specs += [pl.BlockSpec((1, p_len, w), lambda i: (i, 0, 0)) for w in (ATT_KV, ATT_KV, IDX_DIM)]
        args += list(past)
    in_specs += [
        pl.BlockSpec((1, e_new, ATT_KV), lambda i: (i, 0, 0)),
        pl.BlockSpec((1, ATT_KV, e_new), lambda i: (i, 0, 0)),
        pl.BlockSpec((1, e_new, IDX_DIM), lambda i: (i, 0, 0)),
        pl.BlockSpec((1, tq, ATT_Q), lambda i: (i, tile, 0)),
        pl.BlockSpec((1, tq, IDX_Q), lambda i: (i, tile, 0)),
        pl.BlockSpec((1, _IWT_ROWS, tq), lambda i: (i, 0, tile)),
    ]
    args += [k_new, vt_new, ik_new, q_att, iq, iwt]
    kern = functools.partial(_dsa_kernel, p_len=p_len, e_new=e_new, tq=tq, topk=topk,
                             nbits=max(1, (e - 1).bit_length()))
    return pl.pallas_call(
        kern,
        out_shape=jax.ShapeDtypeStruct((b, tq, ATT_Q), BF16),
        grid=(b,),
        in_specs=in_specs,
        out_specs=pl.BlockSpec((1, tq, ATT_Q), lambda i: (i, 0, 0)),
        scratch_shapes=[
            pltpu.VMEM((e, tq), F32),
            pltpu.VMEM((e, tq), I32),
            pltpu.VMEM((e, tq), F32),
            pltpu.VMEM((1, tq), I32),
        ],
        compiler_params=_params("parallel"),
        name=f"dsa_e{e}",
    )(*args)


def _dsa(q_att, iq, iwt, k_new, vt_new, ik_new, past):
    b, l, _ = q_att.shape
    p_len = 0 if past is None else past[0].shape[1]
    topk = min(TOPK_MAX, (p_len + l) // 4)
    if l <= CHUNK:
        return _dsa_call(q_att, iq, iwt, k_new, vt_new, ik_new, past, tq=l, tile=0, topk=topk)
    assert l % _DSA_TQ == 0
    outs = [_dsa_call(q_att, iq, iwt, k_new, vt_new, ik_new, past, tq=_DSA_TQ, tile=t, topk=topk)
            for t in range(l // _DSA_TQ)]
    return jnp.concatenate(outs, axis=1)


def _block_diag_ones(n, blk):
    i = jnp.arange(n) // blk
    return (i[:, None] == i[None, :])


def _prep_weights(w_ada, b_ada, g_ffn1, w1_gate, w1_up, w1_down, g_mix, w_in, w_a2, b_a, g_gla,
                  g_q, g_k, w_out, g_ffn2, w2_gate, w2_up, w2_down, g_out):
    sizes = (GLA_QK, GLA_QK, GLA_V, GLA_LOWRANK, GLA_V, ATT_Q, ATT_KV, ATT_KV, IDX_Q, IDX_DIM, IDX_HEADS)
    pts, acc = [], 0
    for s in sizes[:-1]:
        acc += s
        pts.append(acc)
    gq, gk, gv, ga, gr, aq, ak, av, iq, ik, iw = jnp.split(w_in, pts, axis=1)
    pad = jnp.zeros((D_MODEL, _PACKED_WIDTH - sum(sizes)), w_in.dtype)
    w_packed = jnp.concatenate([gq, gk, gv, gr, aq, ak, av, iq, ik, ga, iw, pad], axis=1).astype(BF16)
    wa2 = jnp.zeros((V7X_LANES, GLA_QK), F32).at[_MISC_GA:_MISC_GA + GLA_LOWRANK].set(w_a2).astype(BF16)
    row = lambda a: a.reshape(1, -1).astype(F32)
    bd_state = jnp.repeat(jnp.repeat(jnp.eye(GLA_HEADS, dtype=F32), GLA_DK, axis=0), GLA_DV, axis=1)
    return dict(
        w_ada=w_ada.astype(BF16), b_ada=row(b_ada),
        g_ffn1=row(g_ffn1), w1=(w1_gate.astype(BF16), w1_up.astype(BF16), w1_down.astype(BF16)),
        g_mix=row(g_mix), w_packed=w_packed, wa2=wa2, b_a=row(b_a),
        g_gla=row(g_gla), gq_t=row(jnp.tile(g_q, ATT_HEADS)), gk_t=row(jnp.tile(g_k, ATT_KV_HEADS)),
        bd_heads=_block_diag_ones(ATT_Q, ATT_HD).astype(BF16), bd_state=bd_state,
        w_out=w_out.astype(BF16),
        g_ffn2=row(g_ffn2), w2=(w2_gate.astype(BF16), w2_up.astype(BF16), w2_down.astype(BF16)),
        g_out=row(g_out),
    )


def _layer(x, mod, past, s0, w):
    b, l, _ = x.shape
    x1 = _ffn(x, mod, w["g_ffn1"], *w["w1"], mod_base=0)
    (qk, gv, sgr, la, q_att, k_new, v_new, vt_new, ik_new, iq, iwt) = _proj(
        x1, mod, w["g_mix"], w["w_packed"], w["wa2"], w["b_a"], w["gq_t"], w["gk_t"], w["bd_heads"])
    gla_out, s_t = _gla(qk, gv, la, sgr, s0, w["g_gla"], w["bd_state"])
    att = _dsa(q_att, iq, iwt, k_new, vt_new, ik_new, past)
    y = _ffn(x1, mod, w["g_ffn2"], *w["w2"], mod_base=6, mix=(gla_out, att, w["w_out"]), g_out=w["g_out"])
    kv_shape = (b, l, ATT_KV_HEADS, ATT_HD)
    return y, k_new.reshape(kv_shape), v_new.reshape(kv_shape), ik_new, s_t


def kernel(x_prompt, x_sample, c_prompt, c_sample, cache_k, cache_v, cache_idx_k, state_gla, w_ada, b_ada, g_ffn1, w1_gate, w1_up, w1_down, g_mix, w_in, w_a2, b_a, g_gla, g_q, g_k, w_out, g_ffn2, w2_gate, w2_up, w2_down, g_out):
    depth = w_ada.shape[0]
    bp = x_prompt.shape[0]
    yp, ys = x_prompt, x_sample
    outs_p, outs_s = [], []
    for layer in range(depth):
        w = _prep_weights(*(t[layer] for t in (
            w_ada, b_ada, g_ffn1, w1_gate, w1_up, w1_down, g_mix, w_in, w_a2, b_a, g_gla,
            g_q, g_k, w_out, g_ffn2, w2_gate, w2_up, w2_down, g_out)))
        mod = _adaln(jnp.concatenate([c_prompt, c_sample], axis=0), w["w_ada"], w["b_ada"])
        mod = mod.reshape(mod.shape[0], 9, D_MODEL)
        ds, pp = cache_k.shape[1], cache_k.shape[2]
        past = (cache_k[layer].reshape(ds, pp, ATT_KV), cache_v[layer].reshape(ds, pp, ATT_KV), cache_idx_k[layer])
        yp, *rest_p = _layer(yp, mod[:bp], None, None, w)
        ys, *rest_s = _layer(ys, mod[bp:], past, state_gla[layer], w)
        outs_p.append(rest_p)
        outs_s.append(rest_s)
    stack = lambda outs, i: jnp.stack([o[i] for o in outs])
    return (yp, ys,
            stack(outs_p, 0), stack(outs_p, 1), stack(outs_p, 2), stack(outs_p, 3),
            stack(outs_s, 0), stack(outs_s, 1), stack(outs_s, 2), stack(outs_s, 3))
```

```python
import functools

import jax
import jax.numpy as jnp
from jax import lax
from jax.experimental import pallas as pl
from jax.experimental.pallas import tpu as pltpu

F32 = jnp.float32
BF16 = jnp.bfloat16
I32 = jnp.int32

D_MODEL = 1024
D_FF = 2816
CHUNK = 64
GLA_HEADS = 4
GLA_DK = 64
GLA_DV = 128
GLA_LOWRANK = 16
GLA_TAU = 16.0
GLA_BLOCK = 16
ATT_HEADS = 8
ATT_KV_HEADS = 2
ATT_HD = 64
IDX_HEADS = 4
IDX_DIM = 64
TOPK_MAX = 256
EPS = 1e-6

GLA_QK = GLA_HEADS * GLA_DK
GLA_V = GLA_HEADS * GLA_DV
ATT_Q = ATT_HEADS * ATT_HD
ATT_KV = ATT_KV_HEADS * ATT_HD
IDX_Q = IDX_HEADS * IDX_DIM
MIX_WIDTH = GLA_V + ATT_Q
ATT_REP = ATT_HEADS // ATT_KV_HEADS

V7X_LANES = 128
V7X_SUBLANES = 8
V7X_VMEM_LIMIT_BYTES = 56 * 1024 * 1024

_SEG_GQK = 0
_SEG_GV = 512
_SEG_GR = 1024
_SEG_AQ = 1536
_SEG_AKV = 2048
_SEG_IQ = 2304
_SEG_MISC = 2560
_PACKED_WIDTH = 2688
_MISC_GA = IDX_DIM
_MISC_IW = IDX_DIM + GLA_LOWRANK
_IWT_ROWS = V7X_SUBLANES

_FF_CHUNK = 256
_GLA_T = 64
_DSA_TQ = 256
_DSA_LOGIT_ROWS_IN_FLIGHT = 4096
_NEG = -1e30
_INT_MIN = -(2 ** 31)
_POS_BIG = 2 ** 30
_CHUNK_SHIFT = CHUNK.bit_length() - 1
_POS_SPLIT = 256
_SLOPE_PIECES = 3
_LOG2E = 1.4426950408889634
_BF16_SUBLANES = 2 * V7X_SUBLANES
assert 1 << _CHUNK_SHIFT == CHUNK

_NT = (((1,), (1,)), ((), ()))


def _dot(a, b):
    return jnp.dot(a, b, preferred_element_type=F32)


def _dot_nt(a, b):
    return lax.dot_general(a, b, _NT, preferred_element_type=F32)


def _rms(x):
    return x * lax.rsqrt(jnp.mean(x * x, axis=-1, keepdims=True) + EPS)


def _silu(x):
    return x * jax.nn.sigmoid(x)


def _col_reduce(x, op, chains=8):
    r, n = x.shape
    while chains > 1 and r % (chains * V7X_SUBLANES):
        chains //= 2
    if r % (chains * V7X_SUBLANES):
        return op(x, axis=0, keepdims=True)
    slab = chains * V7X_SUBLANES
    pair = jnp.add if op is jnp.sum else jnp.maximum
    acc = x[:slab]
    for i in range(1, r // slab):
        acc = pair(acc, x[i * slab:(i + 1) * slab])
    return op(acc, axis=0, keepdims=True)


def _row_tiling(batch, length, target):
    if length >= target:
        assert length % target == 0
        return 1, target
    nb = max(1, min(batch, target // length))
    while batch % nb:
        nb -= 1
    return nb, length


def _const_spec(shape):
    zeros = (0,) * len(shape)
    return pl.BlockSpec(shape, lambda *_: zeros, pipeline_mode=pl.Buffered(1))


def _params(*sem):
    return pltpu.CompilerParams(dimension_semantics=sem, vmem_limit_bytes=V7X_VMEM_LIMIT_BYTES)


def _adaln_kernel(c_ref, w_ref, b_ref, o_ref):
    a = _silu(c_ref[...]).astype(BF16)
    o_ref[...] = _dot(a, w_ref[...]) + b_ref[...]


def _adaln(c, w_ada, b_ada):
    bt, d = c.shape
    n = w_ada.shape[1]
    tn = d
    return pl.pallas_call(
        _adaln_kernel,
        out_shape=jax.ShapeDtypeStruct((bt, n), F32),
        grid=(n // tn,),
        in_specs=[
            pl.BlockSpec((bt, d), lambda j: (0, 0)),
            pl.BlockSpec((d, tn), lambda j: (0, j)),
            pl.BlockSpec((1, tn), lambda j: (0, j)),
        ],
        out_specs=pl.BlockSpec((bt, tn), lambda j: (0, j)),
        compiler_params=_params("arbitrary"),
        name="adaln",
    )(c, w_ada, b_ada)


def _ffn_kernel(*refs, nb, rows, mod_base, with_mix, with_final_norm):
    it = iter(refs)
    x_ref, mod_ref = next(it), next(it)
    if with_mix:
        gla_ref, att_ref, wout_ref = next(it), next(it), next(it)
    g_ref, wg_ref, wu_ref, wd_ref = next(it), next(it), next(it), next(it)
    if with_final_norm:
        gout_ref = next(it)
    o_ref, h_scr, x_scr = next(it), next(it), next(it)
    tm = nb * rows

    x = x_ref[...].reshape(tm, D_MODEL)
    if with_mix:
        gla = gla_ref[...].reshape(tm, GLA_V)
        att = att_ref[...].reshape(tm, ATT_Q)
        mix = _dot(gla, wout_ref[:GLA_V, :]) + _dot(att, wout_ref[GLA_V:, :])
    g = g_ref[...]
    for j in range(nb):
        sl = slice(j * rows, (j + 1) * rows)
        m = mod_ref[j]
        xj = x[sl]
        if with_mix:
            xj = xj + m[5:6] * mix[sl]
        x_scr[sl, :] = xj
        h = (_rms(xj) * g) * (1.0 + m[mod_base + 1:mod_base + 2]) + m[mod_base:mod_base + 1]
        h_scr[sl, :] = h.astype(BF16)

    h = h_scr[...]
    acc = jnp.zeros((tm, D_MODEL), F32)
    for c in range(D_FF // _FF_CHUNK):
        cs = slice(c * _FF_CHUNK, (c + 1) * _FF_CHUNK)
        gate = _dot(h, wg_ref[:, cs])
        up = _dot(h, wu_ref[:, cs])
        a = (_silu(gate) * up).astype(BF16)
        acc = acc + _dot(a, wd_ref[cs, :])

    for j in range(nb):
        sl = slice(j * rows, (j + 1) * rows)
        m = mod_ref[j]
        y = x_scr[sl, :] + 0.5 * m[mod_base + 2:mod_base + 3] * acc[sl]
        if with_final_norm:
            y = _rms(y) * gout_ref[...]
        o_ref[j] = y


def _ffn(x, mod, g, wg, wu, wd, *, mod_base, mix=None, g_out=None, tm=512):
    b, l, d = x.shape
    nb, rows = _row_tiling(b, l, tm)
    grid = (b // nb, l // rows)
    row_spec = lambda w: pl.BlockSpec((nb, rows, w), lambda i, r: (i, r, 0))
    in_specs = [row_spec(d), pl.BlockSpec((nb, 9, d), lambda i, r: (i, 0, 0))]
    args = [x, mod]
    if mix is not None:
        gla, att, wout = mix
        in_specs += [row_spec(GLA_V), row_spec(ATT_Q), _const_spec(wout.shape)]
        args += [gla, att, wout]
    in_specs += [_const_spec(g.shape), _const_spec(wg.shape), _const_spec(wu.shape), _const_spec(wd.shape)]
    args += [g, wg, wu, wd]
    if g_out is not None:
        in_specs.append(_const_spec(g_out.shape))
        args.append(g_out)
    kern = functools.partial(
        _ffn_kernel, nb=nb, rows=rows, mod_base=mod_base,
        with_mix=mix is not None, with_final_norm=g_out is not None)
    return pl.pallas_call(
        kern,
        out_shape=jax.ShapeDtypeStruct((b, l, d), F32),
        grid=grid,
        in_specs=in_specs,
        out_specs=row_spec(d),
        scratch_shapes=[pltpu.VMEM((nb * rows, d), BF16), pltpu.VMEM((nb * rows, d), F32)],
        compiler_params=_params("parallel", "parallel"),
        name="ffn_mix" if mix is not None else "ffn",
    )(*args)


def _proj_kernel(x_ref, mod_ref, g_ref, w_ref, wa2_ref, ba_ref, gq_ref, gk_ref, bd_ref,
                 qk_ref, v_ref, sgr_ref, la_ref, qatt_ref, knew_ref, vnew_ref, vtnew_ref,
                 iknew_ref, iq_ref, iwt_ref, h_scr, *, nb, rows):
    tm = nb * rows
    x = x_ref[...].reshape(tm, D_MODEL)
    g = g_ref[...]
    for j in range(nb):
        sl = slice(j * rows, (j + 1) * rows)
        m = mod_ref[j]
        h = (_rms(x[sl]) * g) * (1.0 + m[4:5]) + m[3:4]
        h_scr[sl, :] = h.astype(BF16)
    h = h_scr[...]

    def seg(start, width):
        return _dot(h, w_ref[:, start:start + width])

    def put(ref, val, width):
        ref[...] = val.reshape(nb, rows, width).astype(ref.dtype)

    lane = lax.broadcasted_iota(I32, (1, 2 * GLA_QK), 1)
    qscale = jnp.where(lane < GLA_QK, GLA_DK ** -0.5, 1.0).astype(F32)
    put(qk_ref, seg(_SEG_GQK, 2 * GLA_QK) * qscale, 2 * GLA_QK)
    put(v_ref, seg(_SEG_GV, GLA_V), GLA_V)
    put(sgr_ref, _silu(seg(_SEG_GR, GLA_V)), GLA_V)

    aq = seg(_SEG_AQ, ATT_Q)
    msq = _dot((aq * aq).astype(BF16), bd_ref[...]) * (1.0 / ATT_HD)
    put(qatt_ref, aq * lax.rsqrt(msq + EPS) * gq_ref[...] * (ATT_HD ** -0.5 * _LOG2E), ATT_Q)

    akv = seg(_SEG_AKV, 2 * ATT_KV)
    ak, av = akv[:, :ATT_KV], akv[:, ATT_KV:]
    msk = _dot((ak * ak).astype(BF16), bd_ref[:ATT_KV, :ATT_KV]) * (1.0 / ATT_HD)
    put(knew_ref, ak * lax.rsqrt(msk + EPS) * gk_ref[...], ATT_KV)
    put(vnew_ref, av, ATT_KV)
    avt = av.T
    for j in range(nb):
        vtnew_ref[j] = avt[:, j * rows:(j + 1) * rows].astype(BF16)

    put(iq_ref, seg(_SEG_IQ, IDX_Q), IDX_Q)

    misc = seg(_SEG_MISC, V7X_LANES)
    put(iknew_ref, misc[:, :IDX_DIM], IDX_DIM)
    za = _dot(misc.astype(BF16), wa2_ref[...]) + ba_ref[...]
    log_sig = jnp.minimum(za, 0.0) - jnp.log(1.0 + jnp.exp(-jnp.abs(za)))
    put(la_ref, log_sig * (1.0 / GLA_TAU), GLA_QK)
    misct = misc.T
    iwt = misct[_MISC_IW:_MISC_IW + _IWT_ROWS, :] * ((IDX_HEADS * IDX_DIM) ** -0.5)
    for j in range(nb):
        iwt_ref[j] = iwt[:, j * rows:(j + 1) * rows]


def _proj(x, mod, g_mix, w_packed, wa2, b_a, gq_t, gk_t, bd, *, tm=512):
    b, l, d = x.shape
    nb, rows = _row_tiling(b, l, tm)
    grid = (b // nb, l // rows)
    row_spec = lambda w: pl.BlockSpec((nb, rows, w), lambda i, r: (i, r, 0))
    col_spec = lambda h: pl.BlockSpec((nb, h, rows), lambda i, r: (i, 0, r))
    sds = lambda w, dt: jax.ShapeDtypeStruct((b, l, w), dt)
    out_shape = [
        sds(2 * GLA_QK, F32), sds(GLA_V, BF16), sds(GLA_V, BF16), sds(GLA_QK, F32), sds(ATT_Q, BF16),
        sds(ATT_KV, F32), sds(ATT_KV, F32), jax.ShapeDtypeStruct((b, ATT_KV, l), BF16),
        sds(IDX_DIM, F32), sds(IDX_Q, F32), jax.ShapeDtypeStruct((b, _IWT_ROWS, l), F32),
    ]
    out_specs = [
        row_spec(2 * GLA_QK), row_spec(GLA_V), row_spec(GLA_V), row_spec(GLA_QK), row_spec(ATT_Q),
        row_spec(ATT_KV), row_spec(ATT_KV), col_spec(ATT_KV),
        row_spec(IDX_DIM), row_spec(IDX_Q), col_spec(_IWT_ROWS),
    ]
    consts = [g_mix, w_packed, wa2, b_a, gq_t, gk_t, bd]
    return pl.pallas_call(
        functools.partial(_proj_kernel, nb=nb, rows=rows),
        out_shape=out_shape,
        grid=grid,
        in_specs=[row_spec(d), pl.BlockSpec((nb, 9, d), lambda i, r: (i, 0, 0))]
        + [_const_spec(c.shape) for c in consts],
        out_specs=out_specs,
        scratch_shapes=[pltpu.VMEM((nb * rows, d), BF16)],
        compiler_params=_params("parallel", "parallel"),
        name="proj",
    )(x, mod, *consts)


def _gla_kernel(*refs, has_s0, nbb):
    it = iter(refs)
    qk_ref, v_ref, la_ref, sgr_ref = next(it), next(it), next(it), next(it)
    s0_ref = next(it) if has_s0 else None
    g_ref, bd_ref, o_ref, st_ref, s_scr = next(it), next(it), next(it), next(it), next(it)
    t, cb = _GLA_T, GLA_BLOCK
    nsb = t // cb
    c = pl.program_id(1)

    @pl.when(c == 0)
    def _():
        s_scr[...] = jnp.zeros((nbb, GLA_QK, GLA_V), F32)
        if has_s0:
            for j in range(nbb):
                for h in range(GLA_HEADS):
                    s_scr[j, h * GLA_DK:(h + 1) * GLA_DK, h * GLA_DV:(h + 1) * GLA_DV] = s0_ref[j, h]

    ri = lax.broadcasted_iota(I32, (t, t), 0)
    ci = lax.broadcasted_iota(I32, (t, t), 1)
    tril = jnp.where(ri >= ci, 1.0, 0.0).astype(BF16)
    lane = lax.broadcasted_iota(I32, (1, GLA_QK), 1)
    head_mask = [jnp.where((lane >= h * GLA_DK) & (lane < (h + 1) * GLA_DK), 1.0, 0.0).astype(F32)
                 for h in range(GLA_HEADS)]
    g = g_ref[...]


    cum = []
    for j in range(nbb):
        la = la_ref[j]
        p0 = la.astype(BF16)
        r0 = la - p0.astype(F32)
        p1 = r0.astype(BF16)
        p2 = (r0 - p1.astype(F32)).astype(BF16)
        bb = _dot(tril, jnp.concatenate([p0, p1, p2], axis=1))
        cum.append(bb[:, :GLA_QK] + bb[:, GLA_QK:2 * GLA_QK] + bb[:, 2 * GLA_QK:])

    o_inter, scores = [], []
    for j in range(nbb):
        b = cum[j]
        qk = qk_ref[j]
        q, k = qk[:, :GLA_QK], qk[:, GLA_QK:]
        v = v_ref[j]
        b0 = [jnp.zeros((1, GLA_QK), F32)] + [b[cb * i - 1:cb * i, :] for i in range(1, nsb)]
        btot = b[t - 1:t, :]
        bstart = jnp.concatenate([jnp.broadcast_to(b0[i], (cb, GLA_QK)) for i in range(nsb)], axis=0)
        q_rel = q * jnp.exp(b - bstart)
        q_int = (q * jnp.exp(b)).astype(BF16)
        k_end = k * jnp.exp(btot - b)

        s_old = s_scr[j]
        o_inter.append(_dot(q_int, s_old.astype(BF16)))
        ds = _dot(k_end.T.astype(BF16), v)
        dcol = jnp.exp(jnp.broadcast_to(btot, (V7X_SUBLANES, GLA_QK)).T[:, 0:1])
        s_scr[j] = s_old * dcol + ds * bd_ref[...]

        sc = []
        for i in range(nsb):
            n = cb * (i + 1)
            qs = q_rel[cb * i:cb * (i + 1)]
            q_stack = jnp.concatenate([qs * head_mask[h] for h in range(GLA_HEADS)], axis=0).astype(BF16)
            km = (k[:n] * jnp.exp(b0[i] - b[:n])).astype(BF16)
            sc.append(_dot_nt(q_stack, km))
        scores.append(sc)

    for j in range(nbb):
        v = v_ref[j]
        intra = []
        for i in range(nsb):
            n = cb * (i + 1)
            rr = lax.broadcasted_iota(I32, (GLA_HEADS * cb, n), 0)
            cc = lax.broadcasted_iota(I32, (GLA_HEADS * cb, n), 1)
            a = jnp.where(cc - cb * i <= (rr & (cb - 1)), scores[j][i], 0.0).astype(BF16)
            oi = _dot(a, v[:n])
            intra.append(jnp.concatenate(
                [oi[h * cb:(h + 1) * cb, h * GLA_DV:(h + 1) * GLA_DV] for h in range(GLA_HEADS)], axis=1))
        o = o_inter[j] + jnp.concatenate(intra, axis=0)
        on = jnp.concatenate([_rms(o[:, h * GLA_DV:(h + 1) * GLA_DV]) * g for h in range(GLA_HEADS)], axis=1)
        o_ref[j] = (on * sgr_ref[j].astype(F32)).astype(o_ref.dtype)

    @pl.when(c == pl.num_programs(1) - 1)
    def _():
        for j in range(nbb):
            for h in range(GLA_HEADS):
                st_ref[j, h] = s_scr[j, h * GLA_DK:(h + 1) * GLA_DK, h * GLA_DV:(h + 1) * GLA_DV]


def _gla(qk, v, la, sgr, s0, g_gla, bd_state, *, nbb=8):
    b, l, _ = qk.shape
    assert l % _GLA_T == 0
    while b % nbb:
        nbb -= 1
    tok = lambda w: pl.BlockSpec((nbb, _GLA_T, w), lambda i, c: (i, c, 0))
    st_spec = pl.BlockSpec((nbb, GLA_HEADS, GLA_DK, GLA_DV), lambda i, c: (i, 0, 0, 0))
    in_specs = [tok(2 * GLA_QK), tok(GLA_V), tok(GLA_QK), tok(GLA_V)]
    args = [qk, v, la, sgr]
    if s0 is not None:
        in_specs.append(st_spec)
        args.append(s0)
    in_specs += [_const_spec(g_gla.shape), _const_spec(bd_state.shape)]
    args += [g_gla, bd_state]
    return pl.pallas_call(
        functools.partial(_gla_kernel, has_s0=s0 is not None, nbb=nbb),
        out_shape=[jax.ShapeDtypeStruct((b, l, GLA_V), BF16),
                   jax.ShapeDtypeStruct((b, GLA_HEADS, GLA_DK, GLA_DV), F32)],
        grid=(b // nbb, l // _GLA_T),
        in_specs=in_specs,
        out_specs=[tok(GLA_V), st_spec],
        scratch_shapes=[pltpu.VMEM((nbb, GLA_QK, GLA_V), F32)],
        compiler_params=_params("parallel", "arbitrary"),
        name="gla",
    )(*args)


def _dsa_kernel(*refs, p_len, e_new, tq, topk, nbits):
    has_past = p_len > 0
    it = iter(refs)
    if has_past:
        kp_ref, vp_ref, ikp_ref = next(it), next(it), next(it)
    kn_ref, vtn_ref, ikn_ref, q_ref, iq_ref, iwt_ref = (next(it) for _ in range(6))
    o_ref, score_scr, aux_scr, bias_scr = (next(it) for _ in range(4))
    e = p_len + e_new
    w0 = e - tq

    k_all, ik_all, vt = kn_ref[0], ikn_ref[0], vtn_ref[0]
    if has_past:
        k_all = jnp.concatenate([kp_ref[0], k_all], axis=0)
        ik_all = jnp.concatenate([ikp_ref[0], ik_all], axis=0)
        vt = jnp.concatenate([vp_ref[0].T.astype(BF16), vt], axis=1)

    def on_window(full, fn):
        return jnp.concatenate([full[:w0], fn(full[w0:])], axis=0) if w0 else fn(full)

    row = lax.broadcasted_iota(I32, (e, tq), 0)
    wr = lax.broadcasted_iota(I32, (tq, tq), 0)
    wc = lax.broadcasted_iota(I32, (tq, tq), 1)
    limit = (((w0 + wc) >> _CHUNK_SHIFT) + 1) << _CHUNK_SHIFT
    adm_win = (w0 + wr) < limit

    ik_hi = ik_all.astype(BF16)
    ik_lo = (ik_all - ik_hi.astype(F32)).astype(BF16)
    ik3 = jnp.concatenate([ik_hi, ik_lo, ik_hi], axis=1)
    iq = iq_ref[0]
    iwt = iwt_ref[0]
    wide = tq % V7X_LANES == 0

    def iq_pieces(h):
        iqh = iq[:, h * IDX_DIM:(h + 1) * IDX_DIM]
        hi = iqh.astype(BF16)
        lo = (iqh - hi.astype(F32)).astype(BF16)
        return jnp.concatenate([hi, hi, lo], axis=1)

    if wide:
        logits = _dot_nt(ik3, jnp.concatenate([iq_pieces(h) for h in range(IDX_HEADS)], axis=0))
        logit = [logits[:, h * tq:(h + 1) * tq] for h in range(IDX_HEADS)]
    else:
        logit = [_dot_nt(ik3, iq_pieces(h)) for h in range(IDX_HEADS)]
    score = jnp.zeros((e, tq), F32)
    for h in range(IDX_HEADS):
        score = score + iwt[h:h + 1, :] * jnp.maximum(logit[h], 0.0)
    score_scr[...] = on_window(score, lambda s: jnp.where(adm_win, s, -jnp.inf))

    def key_to_f32(c):
        return lax.bitcast_convert_type(c ^ ((c >> 31) & 0x7FFFFFFF), F32)

    def count_ge(cand):
        cf = key_to_f32(cand)
        slab = 8 * V7X_SUBLANES
        if e % slab:
            return _col_reduce(jnp.where(score_scr[...] >= cf, 1.0, 0.0), jnp.sum)
        acc = jnp.zeros((slab, tq), F32)
        for i in range(e // slab):
            acc = jnp.where(score_scr[i * slab:(i + 1) * slab, :] >= cf, acc + 1.0, acc)
        return jnp.sum(acc, axis=0, keepdims=True)

    c0 = count_ge(jnp.zeros((1, tq), I32))
    thr = jnp.where(c0 >= topk, 0, _INT_MIN).astype(I32)
    cnt = jnp.where(c0 >= topk, c0, float(e))

    def thr_step(i, carry):
        thr, cnt = carry
        cand = thr | lax.shift_left(jnp.int32(1), 30 - i)
        c = count_ge(cand)
        ok = c >= topk
        return jnp.where(ok, cand, thr), jnp.where(ok, c, cnt)

    thr, cnt = lax.fori_loop(0, 31, thr_step, (thr, cnt))
    thr_f = jnp.where(thr == _INT_MIN, -jnp.inf, key_to_f32(thr))

    def store_bias(sel_bias):
        bias_scr[...] = on_window(sel_bias, lambda s: jnp.where(adm_win, s, _NEG))

    store_bias(jnp.where(score_scr[...] >= thr_f, 0.0, _NEG))

    @pl.when(jnp.max(cnt) > topk)
    def _():
        sc = score_scr[...]
        need = topk - _col_reduce(jnp.where(sc > thr_f, 1.0, 0.0), jnp.sum)
        aux_scr[...] = jnp.where(sc == thr_f, row, _POS_BIG)

        def tie_step(i, x):
            cand = x | lax.shift_left(jnp.int32(1), nbits - 1 - i)
            below = _col_reduce(jnp.where(aux_scr[...] < cand, 1.0, 0.0), jnp.sum)
            return jnp.where(below < need, cand, x)

        x = lax.fori_loop(0, nbits, tie_step, jnp.zeros((1, tq), I32))
        store_bias(jnp.where(sc > thr_f, 0.0, jnp.where(aux_scr[...] <= x, 0.0, _NEG)))

    assert e <= _POS_SPLIT * _POS_SPLIT
    pr = lax.broadcasted_iota(I32, (e, ATT_HD), 0)
    pc = lax.broadcasted_iota(I32, (e, ATT_HD), 1)
    pos_lo = pr & (_POS_SPLIT - 1)
    pos_cols = jnp.where(pc < 2 * _SLOPE_PIECES, jnp.where((pc & 1) == 0, pr - pos_lo, pos_lo), 0)
    pos_cols = pos_cols.astype(F32).astype(BF16)
    k_aug = [jnp.concatenate([k_all[:, g * ATT_HD:(g + 1) * ATT_HD].astype(BF16), pos_cols], axis=1)
             for g in range(ATT_KV_HEADS)]
    ahead = -2.0 * jnp.maximum(wr - wc, 0).astype(F32)
    qc = lax.broadcasted_iota(I32, (tq, ATT_HD), 1)
    q = q_ref[0]

    def slope_of(h):
        return jnp.float32(_LOG2E * 2.0 ** (-8.0 * (h + 1) / ATT_HEADS))

    def q_aug(h):
        rest = jnp.full((tq, ATT_HD), slope_of(h), F32)
        cols = jnp.zeros((tq, ATT_HD), F32)
        for i in range(_SLOPE_PIECES):
            piece = rest.astype(BF16).astype(F32)
            rest = rest - piece
            cols = jnp.where((qc >> 1) == i, piece, cols)
        return jnp.concatenate([q[:, h * ATT_HD:(h + 1) * ATT_HD], cols.astype(BF16)], axis=1)

    def probs_of(s, h):
        s = on_window(s, lambda sw: sw + slope_of(h) * ahead) + bias_scr[...]
        return jnp.exp2(s - _col_reduce(s, jnp.max)).astype(BF16)

    ones_rows = jnp.ones((_BF16_SUBLANES, e), BF16)
    vt_aug = [jnp.concatenate([vt[g * ATT_HD:(g + 1) * ATT_HD, :], ones_rows], axis=0) for g in range(ATT_KV_HEADS)]

    def normalised(ot):
        return ot[:ATT_HD] / ot[ATT_HD:ATT_HD + 1]

    outs = []
    if wide:
        group_logits = [
            _dot_nt(k_aug[g], jnp.concatenate([q_aug(g * ATT_REP + r) for r in range(ATT_REP)], axis=0))
            for g in range(ATT_KV_HEADS)]
        for g in range(ATT_KV_HEADS):
            probs = [probs_of(group_logits[g][:, r * tq:(r + 1) * tq], g * ATT_REP + r) for r in range(ATT_REP)]
            ot = _dot(vt_aug[g], jnp.concatenate(probs, axis=1))
            outs += [normalised(ot[:, r * tq:(r + 1) * tq]) for r in range(ATT_REP)]
    else:
        ahead_heads = max(1, min(ATT_HEADS - 1, _DSA_LOGIT_ROWS_IN_FLIGHT // e))
        qk = lambda h: _dot_nt(k_aug[h // ATT_REP], q_aug(h))
        pending = [qk(h) for h in range(ahead_heads)]
        for h in range(ATT_HEADS):
            s = pending.pop(0)
            if h + ahead_heads < ATT_HEADS:
                pending.append(qk(h + ahead_heads))
            outs.append(normalised(_dot(vt_aug[h // ATT_REP], probs_of(s, h))))
    ot = jnp.concatenate(outs, axis=0)
    if tq % V7X_LANES:
        ot = jnp.concatenate([ot, jnp.zeros((ATT_Q, V7X_LANES - tq % V7X_LANES), F32)], axis=1)
    o_ref[0] = ot.T[:tq].astype(o_ref.dtype)


def _dsa_call(q_att, iq, iwt, k_new, vt_new, ik_new, past, *, tq, tile, topk):
    b = q_att.shape[0]
    p_len = 0 if past is None else past[0].shape[1]
    e_new = (tile + 1) * tq
    e = p_len + e_new
    in_specs, args = [], []
    if past is not None:
        in_specs += [pl.BlockSpec((1, p_len, w), lambda i: (i, 0, 0)) for w in (ATT_KV, ATT_KV, IDX_DIM)]
        args += list(past)
    in_specs += [
        pl.BlockSpec((1, e_new, ATT_KV), lambda i: (i, 0, 0)),
        pl.BlockSpec((1, ATT_KV, e_new), lambda i: (i, 0, 0)),
        pl.BlockSpec((1, e_new, IDX_DIM), lambda i: (i, 0, 0)),
        pl.BlockSpec((1, tq, ATT_Q), lambda i: (i, tile, 0)),
        pl.BlockSpec((1, tq, IDX_Q), lambda i: (i, tile, 0)),
        pl.BlockSpec((1, _IWT_ROWS, tq), lambda i: (i, 0, tile)),
    ]
    args += [k_new, vt_new, ik_new, q_att, iq, iwt]
    kern = functools.partial(_dsa_kernel, p_len=p_len, e_new=e_new, tq=tq, topk=topk,
                             nbits=max(1, (e - 1).bit_length()))
    return pl.pallas_call(
        kern,
        out_shape=jax.ShapeDtypeStruct((b, tq, ATT_Q), BF16),
        grid=(b,),
        in_specs=in_specs,
        out_specs=pl.BlockSpec((1, tq, ATT_Q), lambda i: (i, 0, 0)),
        scratch_shapes=[
            pltpu.VMEM((e, tq), F32),
            pltpu.VMEM((e, tq), I32),
            pltpu.VMEM((e, tq), F32),
        ],
        compiler_params=_params("parallel"),
        name=f"dsa_e{e}",
    )(*args)


def _dsa(q_att, iq, iwt, k_new, vt_new, ik_new, past):
    b, l, _ = q_att.shape
    p_len = 0 if past is None else past[0].shape[1]
    topk = min(TOPK_MAX, (p_len + l) // 4)
    if l <= CHUNK:
        return _dsa_call(q_att, iq, iwt, k_new, vt_new, ik_new, past, tq=l, tile=0, topk=topk)
    assert l % _DSA_TQ == 0
    outs = [_dsa_call(q_att, iq, iwt, k_new, vt_new, ik_new, past, tq=_DSA_TQ, tile=t, topk=topk)
            for t in range(l // _DSA_TQ)]
    return jnp.concatenate(outs, axis=1)


def _block_diag_ones(n, blk):
    i = jnp.arange(n) // blk
    return (i[:, None] == i[None, :])


def _prep_weights(w_ada, b_ada, g_ffn1, w1_gate, w1_up, w1_down, g_mix, w_in, w_a2, b_a, g_gla,
                  g_q, g_k, w_out, g_ffn2, w2_gate, w2_up, w2_down, g_out):
    sizes = (GLA_QK, GLA_QK, GLA_V, GLA_LOWRANK, GLA_V, ATT_Q, ATT_KV, ATT_KV, IDX_Q, IDX_DIM, IDX_HEADS)
    pts, acc = [], 0
    for s in sizes[:-1]:
        acc += s
        pts.append(acc)
    gq, gk, gv, ga, gr, aq, ak, av, iq, ik, iw = jnp.split(w_in, pts, axis=1)
    pad = jnp.zeros((D_MODEL, _PACKED_WIDTH - sum(sizes)), w_in.dtype)
    w_packed = jnp.concatenate([gq, gk, gv, gr, aq, ak, av, iq, ik, ga, iw, pad], axis=1).astype(BF16)
    wa2 = jnp.zeros((V7X_LANES, GLA_QK), F32).at[_MISC_GA:_MISC_GA + GLA_LOWRANK].set(w_a2).astype(BF16)
    row = lambda a: a.reshape(1, -1).astype(F32)
    bd_state = jnp.repeat(jnp.repeat(jnp.eye(GLA_HEADS, dtype=F32), GLA_DK, axis=0), GLA_DV, axis=1)
    return dict(
        w_ada=w_ada.astype(BF16), b_ada=row(b_ada),
        g_ffn1=row(g_ffn1), w1=(w1_gate.astype(BF16), w1_up.astype(BF16), w1_down.astype(BF16)),
        g_mix=row(g_mix), w_packed=w_packed, wa2=wa2, b_a=row(b_a),
        g_gla=row(g_gla), gq_t=row(jnp.tile(g_q, ATT_HEADS)), gk_t=row(jnp.tile(g_k, ATT_KV_HEADS)),
        bd_heads=_block_diag_ones(ATT_Q, ATT_HD).astype(BF16), bd_state=bd_state,
        w_out=w_out.astype(BF16),
        g_ffn2=row(g_ffn2), w2=(w2_gate.astype(BF16), w2_up.astype(BF16), w2_down.astype(BF16)),
        g_out=row(g_out),
    )


def _layer(x, mod, past, s0, w):
    b, l, _ = x.shape
    x1 = _ffn(x, mod, w["g_ffn1"], *w["w1"], mod_base=0)
    (qk, gv, sgr, la, q_att, k_new, v_new, vt_new, ik_new, iq, iwt) = _proj(
        x1, mod, w["g_mix"], w["w_packed"], w["wa2"], w["b_a"], w["gq_t"], w["gk_t"], w["bd_heads"])
    gla_out, s_t = _gla(qk, gv, la, sgr, s0, w["g_gla"], w["bd_state"])
    att = _dsa(q_att, iq, iwt, k_new, vt_new, ik_new, past)
    y = _ffn(x1, mod, w["g_ffn2"], *w["w2"], mod_base=6, mix=(gla_out, att, w["w_out"]), g_out=w["g_out"])
    kv_shape = (b, l, ATT_KV_HEADS, ATT_HD)
    return y, k_new.reshape(kv_shape), v_new.reshape(kv_shape), ik_new, s_t


def kernel(x_prompt, x_sample, c_prompt, c_sample, cache_k, cache_v, cache_idx_k, state_gla, w_ada, b_ada, g_ffn1, w1_gate, w1_up, w1_down, g_mix, w_in, w_a2, b_a, g_gla, g_q, g_k, w_out, g_ffn2, w2_gate, w2_up, w2_down, g_out):
    depth = w_ada.shape[0]
    bp = x_prompt.shape[0]
    yp, ys = x_prompt, x_sample
    outs_p, outs_s = [], []
    for layer in range(depth):
        w = _prep_weights(*(t[layer] for t in (
            w_ada, b_ada, g_ffn1, w1_gate, w1_up, w1_down, g_mix, w_in, w_a2, b_a, g_gla,
            g_q, g_k, w_out, g_ffn2, w2_gate, w2_up, w2_down, g_out)))
        mod = _adaln(jnp.concatenate([c_prompt, c_sample], axis=0), w["w_ada"], w["b_ada"])
        mod = mod.reshape(mod.shape[0], 9, D_MODEL)
        ds, pp = cache_k.shape[1], cache_k.shape[2]
        past = (cache_k[layer].reshape(ds, pp, ATT_KV), cache_v[layer].reshape(ds, pp, ATT_KV), cache_idx_k[layer])
        yp, *rest_p = _layer(yp, mod[:bp], None, None, w)
        ys, *rest_s = _layer(ys, mod[bp:], past, state_gla[layer], w)
        outs_p.append(rest_p)
        outs_s.append(rest_s)
    stack = lambda outs, i: jnp.stack([o[i] for o in outs])
    return (yp, ys,
            stack(outs_p, 0), stack(outs_p, 1), stack(outs_p, 2), stack(outs_p, 3),
            stack(outs_s, 0), stack(outs_s, 1), stack(outs_s, 2), stack(outs_s, 3))
```

```python
import functools

import jax
import jax.numpy as jnp
from jax import lax
from jax.experimental import pallas as pl
from jax.experimental.pallas import tpu as pltpu

F32 = jnp.float32
BF16 = jnp.bfloat16
I32 = jnp.int32

D_MODEL = 1024
D_FF = 2816
CHUNK = 64
GLA_HEADS = 4
GLA_DK = 64
GLA_DV = 128
GLA_LOWRANK = 16
GLA_TAU = 16.0
GLA_BLOCK = 16
ATT_HEADS = 8
ATT_KV_HEADS = 2
ATT_HD = 64
IDX_HEADS = 4
IDX_DIM = 64
TOPK_MAX = 256
EPS = 1e-6

GLA_QK = GLA_HEADS * GLA_DK
GLA_V = GLA_HEADS * GLA_DV
ATT_Q = ATT_HEADS * ATT_HD
ATT_KV = ATT_KV_HEADS * ATT_HD
IDX_Q = IDX_HEADS * IDX_DIM
MIX_WIDTH = GLA_V + ATT_Q
ATT_REP = ATT_HEADS // ATT_KV_HEADS

V7X_LANES = 128
V7X_SUBLANES = 8
V7X_VMEM_LIMIT_BYTES = 56 * 1024 * 1024

_SEG_GQK = 0
_SEG_GV = 512
_SEG_GR = 1024
_SEG_AQ = 1536
_SEG_AKV = 2048
_SEG_IQ = 2304
_SEG_MISC = 2560
_PACKED_WIDTH = 2688
_MISC_GA = IDX_DIM
_MISC_IW = IDX_DIM + GLA_LOWRANK
_IWT_ROWS = V7X_SUBLANES

_FF_CHUNK = 256
_GLA_T = 64
_DSA_TQ = 256
_DSA_LOGIT_ROWS_IN_FLIGHT = 4096
_NEG = -1e30
_INT_MIN = -(2 ** 31)
_CHUNK_SHIFT = CHUNK.bit_length() - 1
_POS_SPLIT = 256
_SLOPE_PIECES = 3
_LOG2E = 1.4426950408889634
_BF16_SUBLANES = 2 * V7X_SUBLANES
assert 1 << _CHUNK_SHIFT == CHUNK

_NT = (((1,), (1,)), ((), ()))


def _dot(a, b):
    return jnp.dot(a, b, preferred_element_type=F32)


def _dot_nt(a, b):
    return lax.dot_general(a, b, _NT, preferred_element_type=F32)


def _rms(x):
    return x * lax.rsqrt(jnp.mean(x * x, axis=-1, keepdims=True) + EPS)


def _silu(x):
    return x * jax.nn.sigmoid(x)


def _col_reduce(x, op, chains=8):
    r, n = x.shape
    while chains > 1 and r % (chains * V7X_SUBLANES):
        chains //= 2
    if r % (chains * V7X_SUBLANES):
        return op(x, axis=0, keepdims=True)
    slab = chains * V7X_SUBLANES
    pair = jnp.add if op is jnp.sum else jnp.maximum
    acc = x[:slab]
    for i in range(1, r // slab):
        acc = pair(acc, x[i * slab:(i + 1) * slab])
    return op(acc, axis=0, keepdims=True)


def _row_tiling(batch, length, target):
    if length >= target:
        assert length % target == 0
        return 1, target
    nb = max(1, min(batch, target // length))
    while batch % nb:
        nb -= 1
    return nb, length


def _const_spec(shape):
    zeros = (0,) * len(shape)
    return pl.BlockSpec(shape, lambda *_: zeros, pipeline_mode=pl.Buffered(1))


def _params(*sem):
    return pltpu.CompilerParams(dimension_semantics=sem, vmem_limit_bytes=V7X_VMEM_LIMIT_BYTES)


def _adaln_kernel(c_ref, w_ref, b_ref, o_ref):
    a = _silu(c_ref[...]).astype(BF16)
    o_ref[...] = _dot(a, w_ref[...]) + b_ref[...]


def _adaln(c, w_ada, b_ada):
    bt, d = c.shape
    n = w_ada.shape[1]
    tn = d
    return pl.pallas_call(
        _adaln_kernel,
        out_shape=jax.ShapeDtypeStruct((bt, n), F32),
        grid=(n // tn,),
        in_specs=[
            pl.BlockSpec((bt, d), lambda j: (0, 0)),
            pl.BlockSpec((d, tn), lambda j: (0, j)),
            pl.BlockSpec((1, tn), lambda j: (0, j)),
        ],
        out_specs=pl.BlockSpec((bt, tn), lambda j: (0, j)),
        compiler_params=_params("arbitrary"),
        name="adaln",
    )(c, w_ada, b_ada)


def _ffn_kernel(*refs, nb, rows, mod_base, with_mix, with_final_norm):
    it = iter(refs)
    x_ref, mod_ref = next(it), next(it)
    if with_mix:
        gla_ref, att_ref, wout_ref = next(it), next(it), next(it)
    g_ref, wg_ref, wu_ref, wd_ref = next(it), next(it), next(it), next(it)
    if with_final_norm:
        gout_ref = next(it)
    o_ref, h_scr, x_scr = next(it), next(it), next(it)
    tm = nb * rows

    x = x_ref[...].reshape(tm, D_MODEL)
    if with_mix:
        gla = gla_ref[...].reshape(tm, GLA_V)
        att = att_ref[...].reshape(tm, ATT_Q)
        mix = _dot(gla, wout_ref[:GLA_V, :]) + _dot(att, wout_ref[GLA_V:, :])
    g = g_ref[...]
    for j in range(nb):
        sl = slice(j * rows, (j + 1) * rows)
        m = mod_ref[j]
        xj = x[sl]
        if with_mix:
            xj = xj + m[5:6] * mix[sl]
        x_scr[sl, :] = xj
        h = (_rms(xj) * g) * (1.0 + m[mod_base + 1:mod_base + 2]) + m[mod_base:mod_base + 1]
        h_scr[sl, :] = h.astype(BF16)

    h = h_scr[...]
    acc = jnp.zeros((tm, D_MODEL), F32)
    for c in range(D_FF // _FF_CHUNK):
        cs = slice(c * _FF_CHUNK, (c + 1) * _FF_CHUNK)
        gate = _dot(h, wg_ref[:, cs])
        up = _dot(h, wu_ref[:, cs])
        a = (_silu(gate) * up).astype(BF16)
        acc = acc + _dot(a, wd_ref[cs, :])

    for j in range(nb):
        sl = slice(j * rows, (j + 1) * rows)
        m = mod_ref[j]
        y = x_scr[sl, :] + 0.5 * m[mod_base + 2:mod_base + 3] * acc[sl]
        if with_final_norm:
            y = _rms(y) * gout_ref[...]
        o_ref[j] = y


def _ffn(x, mod, g, wg, wu, wd, *, mod_base, mix=None, g_out=None, tm=512):
    b, l, d = x.shape
    nb, rows = _row_tiling(b, l, tm)
    grid = (b // nb, l // rows)
    row_spec = lambda w: pl.BlockSpec((nb, rows, w), lambda i, r: (i, r, 0))
    in_specs = [row_spec(d), pl.BlockSpec((nb, 9, d), lambda i, r: (i, 0, 0))]
    args = [x, mod]
    if mix is not None:
        gla, att, wout = mix
        in_specs += [row_spec(GLA_V), row_spec(ATT_Q), _const_spec(wout.shape)]
        args += [gla, att, wout]
    in_specs += [_const_spec(g.shape), _const_spec(wg.shape), _const_spec(wu.shape), _const_spec(wd.shape)]
    args += [g, wg, wu, wd]
    if g_out is not None:
        in_specs.append(_const_spec(g_out.shape))
        args.append(g_out)
    kern = functools.partial(
        _ffn_kernel, nb=nb, rows=rows, mod_base=mod_base,
        with_mix=mix is not None, with_final_norm=g_out is not None)
    return pl.pallas_call(
        kern,
        out_shape=jax.ShapeDtypeStruct((b, l, d), F32),
        grid=grid,
        in_specs=in_specs,
        out_specs=row_spec(d),
        scratch_shapes=[pltpu.VMEM((nb * rows, d), BF16), pltpu.VMEM((nb * rows, d), F32)],
        compiler_params=_params("parallel", "parallel"),
        name="ffn_mix" if mix is not None else "ffn",
    )(*args)


def _proj_kernel(x_ref, mod_ref, g_ref, w_ref, wa2_ref, ba_ref, gq_ref, gk_ref, bd_ref,
                 qk_ref, v_ref, sgr_ref, la_ref, qatt_ref, knew_ref, vnew_ref, vtnew_ref,
                 iknew_ref, iq_ref, iwt_ref, h_scr, *, nb, rows):
    tm = nb * rows
    x = x_ref[...].reshape(tm, D_MODEL)
    g = g_ref[...]
    for j in range(nb):
        sl = slice(j * rows, (j + 1) * rows)
        m = mod_ref[j]
        h = (_rms(x[sl]) * g) * (1.0 + m[4:5]) + m[3:4]
        h_scr[sl, :] = h.astype(BF16)
    h = h_scr[...]

    def seg(start, width):
        return _dot(h, w_ref[:, start:start + width])

    def put(ref, val, width):
        ref[...] = val.reshape(nb, rows, width).astype(ref.dtype)

    lane = lax.broadcasted_iota(I32, (1, 2 * GLA_QK), 1)
    qscale = jnp.where(lane < GLA_QK, GLA_DK ** -0.5, 1.0).astype(F32)
    put(qk_ref, seg(_SEG_GQK, 2 * GLA_QK) * qscale, 2 * GLA_QK)
    put(v_ref, seg(_SEG_GV, GLA_V), GLA_V)
    put(sgr_ref, _silu(seg(_SEG_GR, GLA_V)), GLA_V)

    aq = seg(_SEG_AQ, ATT_Q)
    msq = _dot((aq * aq).astype(BF16), bd_ref[...]) * (1.0 / ATT_HD)
    put(qatt_ref, aq * lax.rsqrt(msq + EPS) * gq_ref[...] * (ATT_HD ** -0.5 * _LOG2E), ATT_Q)

    akv = seg(_SEG_AKV, 2 * ATT_KV)
    ak, av = akv[:, :ATT_KV], akv[:, ATT_KV:]
    msk = _dot((ak * ak).astype(BF16), bd_ref[:ATT_KV, :ATT_KV]) * (1.0 / ATT_HD)
    put(knew_ref, ak * lax.rsqrt(msk + EPS) * gk_ref[...], ATT_KV)
    put(vnew_ref, av, ATT_KV)
    avt = av.T
    for j in range(nb):
        vtnew_ref[j] = avt[:, j * rows:(j + 1) * rows].astype(BF16)

    put(iq_ref, seg(_SEG_IQ, IDX_Q), IDX_Q)

    misc = seg(_SEG_MISC, V7X_LANES)
    put(iknew_ref, misc[:, :IDX_DIM], IDX_DIM)
    za = _dot(misc.astype(BF16), wa2_ref[...]) + ba_ref[...]
    log_sig = jnp.minimum(za, 0.0) - jnp.log(1.0 + jnp.exp(-jnp.abs(za)))
    put(la_ref, log_sig * (1.0 / GLA_TAU), GLA_QK)
    misct = misc.T
    iwt = misct[_MISC_IW:_MISC_IW + _IWT_ROWS, :] * ((IDX_HEADS * IDX_DIM) ** -0.5)
    for j in range(nb):
        iwt_ref[j] = iwt[:, j * rows:(j + 1) * rows]


def _proj(x, mod, g_mix, w_packed, wa2, b_a, gq_t, gk_t, bd, *, tm=512):
    b, l, d = x.shape
    nb, rows = _row_tiling(b, l, tm)
    grid = (b // nb, l // rows)
    row_spec = lambda w: pl.BlockSpec((nb, rows, w), lambda i, r: (i, r, 0))
    col_spec = lambda h: pl.BlockSpec((nb, h, rows), lambda i, r: (i, 0, r))
    sds = lambda w, dt: jax.ShapeDtypeStruct((b, l, w), dt)
    out_shape = [
        sds(2 * GLA_QK, F32), sds(GLA_V, BF16), sds(GLA_V, BF16), sds(GLA_QK, F32), sds(ATT_Q, BF16),
        sds(ATT_KV, F32), sds(ATT_KV, F32), jax.ShapeDtypeStruct((b, ATT_KV, l), BF16),
        sds(IDX_DIM, F32), sds(IDX_Q, F32), jax.ShapeDtypeStruct((b, _IWT_ROWS, l), F32),
    ]
    out_specs = [
        row_spec(2 * GLA_QK), row_spec(GLA_V), row_spec(GLA_V), row_spec(GLA_QK), row_spec(ATT_Q),
        row_spec(ATT_KV), row_spec(ATT_KV), col_spec(ATT_KV),
        row_spec(IDX_DIM), row_spec(IDX_Q), col_spec(_IWT_ROWS),
    ]
    consts = [g_mix, w_packed, wa2, b_a, gq_t, gk_t, bd]
    return pl.pallas_call(
        functools.partial(_proj_kernel, nb=nb, rows=rows),
        out_shape=out_shape,
        grid=grid,
        in_specs=[row_spec(d), pl.BlockSpec((nb, 9, d), lambda i, r: (i, 0, 0))]
        + [_const_spec(c.shape) for c in consts],
        out_specs=out_specs,
        scratch_shapes=[pltpu.VMEM((nb * rows, d), BF16)],
        compiler_params=_params("parallel", "parallel"),
        name="proj",
    )(x, mod, *consts)


def _gla_kernel(*refs, has_s0, nbb):
    it = iter(refs)
    qk_ref, v_ref, la_ref, sgr_ref = next(it), next(it), next(it), next(it)
    s0_ref = next(it) if has_s0 else None
    g_ref, bd_ref, o_ref, st_ref, s_scr = next(it), next(it), next(it), next(it), next(it)
    t, cb = _GLA_T, GLA_BLOCK
    nsb = t // cb
    c = pl.program_id(1)

    @pl.when(c == 0)
    def _():
        s_scr[...] = jnp.zeros((nbb, GLA_QK, GLA_V), F32)
        if has_s0:
            for j in range(nbb):
                for h in range(GLA_HEADS):
                    s_scr[j, h * GLA_DK:(h + 1) * GLA_DK, h * GLA_DV:(h + 1) * GLA_DV] = s0_ref[j, h]

    ri = lax.broadcasted_iota(I32, (t, t), 0)
    ci = lax.broadcasted_iota(I32, (t, t), 1)
    tril = jnp.where(ri >= ci, 1.0, 0.0).astype(BF16)
    lane = lax.broadcasted_iota(I32, (1, GLA_QK), 1)
    head_mask = [jnp.where((lane >= h * GLA_DK) & (lane < (h + 1) * GLA_DK), 1.0, 0.0).astype(F32)
                 for h in range(GLA_HEADS)]
    g = g_ref[...]


    cum = []
    for j in range(nbb):
        la = la_ref[j]
        p0 = la.astype(BF16)
        r0 = la - p0.astype(F32)
        p1 = r0.astype(BF16)
        p2 = (r0 - p1.astype(F32)).astype(BF16)
        bb = _dot(tril, jnp.concatenate([p0, p1, p2], axis=1))
        cum.append(bb[:, :GLA_QK] + bb[:, GLA_QK:2 * GLA_QK] + bb[:, 2 * GLA_QK:])

    o_inter, scores = [], []
    for j in range(nbb):
        b = cum[j]
        qk = qk_ref[j]
        q, k = qk[:, :GLA_QK], qk[:, GLA_QK:]
        v = v_ref[j]
        b0 = [jnp.zeros((1, GLA_QK), F32)] + [b[cb * i - 1:cb * i, :] for i in range(1, nsb)]
        btot = b[t - 1:t, :]
        bstart = jnp.concatenate([jnp.broadcast_to(b0[i], (cb, GLA_QK)) for i in range(nsb)], axis=0)
        q_rel = q * jnp.exp(b - bstart)
        q_int = (q * jnp.exp(b)).astype(BF16)
        k_end = k * jnp.exp(btot - b)

        s_old = s_scr[j]
        o_inter.append(_dot(q_int, s_old.astype(BF16)))
        ds = _dot(k_end.T.astype(BF16), v)
        dcol = jnp.exp(jnp.broadcast_to(btot, (V7X_SUBLANES, GLA_QK)).T[:, 0:1])
        s_scr[j] = s_old * dcol + ds * bd_ref[...]

        sc = []
        for i in range(nsb):
            n = cb * (i + 1)
            qs = q_rel[cb * i:cb * (i + 1)]
            q_stack = jnp.concatenate([qs * head_mask[h] for h in range(GLA_HEADS)], axis=0).astype(BF16)
            km = (k[:n] * jnp.exp(b0[i] - b[:n])).astype(BF16)
            sc.append(_dot_nt(q_stack, km))
        scores.append(sc)

    for j in range(nbb):
        v = v_ref[j]
        intra = []
        for i in range(nsb):
            n = cb * (i + 1)
            rr = lax.broadcasted_iota(I32, (GLA_HEADS * cb, n), 0)
            cc = lax.broadcasted_iota(I32, (GLA_HEADS * cb, n), 1)
            a = jnp.where(cc - cb * i <= (rr & (cb - 1)), scores[j][i], 0.0).astype(BF16)
            oi = _dot(a, v[:n])
            intra.append(jnp.concatenate(
                [oi[h * cb:(h + 1) * cb, h * GLA_DV:(h + 1) * GLA_DV] for h in range(GLA_HEADS)], axis=1))
        o = o_inter[j] + jnp.concatenate(intra, axis=0)
        on = jnp.concatenate([_rms(o[:, h * GLA_DV:(h + 1) * GLA_DV]) * g for h in range(GLA_HEADS)], axis=1)
        o_ref[j] = (on * sgr_ref[j].astype(F32)).astype(o_ref.dtype)

    @pl.when(c == pl.num_programs(1) - 1)
    def _():
        for j in range(nbb):
            for h in range(GLA_HEADS):
                st_ref[j, h] = s_scr[j, h * GLA_DK:(h + 1) * GLA_DK, h * GLA_DV:(h + 1) * GLA_DV]


def _gla(qk, v, la, sgr, s0, g_gla, bd_state, *, nbb=8):
    b, l, _ = qk.shape
    assert l % _GLA_T == 0
    while b % nbb:
        nbb -= 1
    tok = lambda w: pl.BlockSpec((nbb, _GLA_T, w), lambda i, c: (i, c, 0))
    st_spec = pl.BlockSpec((nbb, GLA_HEADS, GLA_DK, GLA_DV), lambda i, c: (i, 0, 0, 0))
    in_specs = [tok(2 * GLA_QK), tok(GLA_V), tok(GLA_QK), tok(GLA_V)]
    args = [qk, v, la, sgr]
    if s0 is not None:
        in_specs.append(st_spec)
        args.append(s0)
    in_specs += [_const_spec(g_gla.shape), _const_spec(bd_state.shape)]
    args += [g_gla, bd_state]
    return pl.pallas_call(
        functools.partial(_gla_kernel, has_s0=s0 is not None, nbb=nbb),
        out_shape=[jax.ShapeDtypeStruct((b, l, GLA_V), BF16),
                   jax.ShapeDtypeStruct((b, GLA_HEADS, GLA_DK, GLA_DV), F32)],
        grid=(b // nbb, l // _GLA_T),
        in_specs=in_specs,
        out_specs=[tok(GLA_V), st_spec],
        scratch_shapes=[pltpu.VMEM((nbb, GLA_QK, GLA_V), F32)],
        compiler_params=_params("parallel", "arbitrary"),
        name="gla",
    )(*args)


def _dsa_kernel(*refs, p_len, e_new, tq, topk):
    has_past = p_len > 0
    it = iter(refs)
    if has_past:
        kp_ref, vp_ref, ikp_ref = next(it), next(it), next(it)
    kn_ref, vtn_ref, ikn_ref, q_ref, iq_ref, iwt_ref = (next(it) for _ in range(6))
    o_ref, score_scr, bias_scr = (next(it) for _ in range(3))
    e = p_len + e_new
    w0 = e - tq

    k_all, ik_all, vt = kn_ref[0], ikn_ref[0], vtn_ref[0]
    if has_past:
        k_all = jnp.concatenate([kp_ref[0], k_all], axis=0)
        ik_all = jnp.concatenate([ikp_ref[0], ik_all], axis=0)
        vt = jnp.concatenate([vp_ref[0].T.astype(BF16), vt], axis=1)

    def on_window(full, fn):
        return jnp.concatenate([full[:w0], fn(full[w0:])], axis=0) if w0 else fn(full)

    wr = lax.broadcasted_iota(I32, (tq, tq), 0)
    wc = lax.broadcasted_iota(I32, (tq, tq), 1)
    limit = (((w0 + wc) >> _CHUNK_SHIFT) + 1) << _CHUNK_SHIFT
    adm_win = (w0 + wr) < limit

    ik_hi = ik_all.astype(BF16)
    ik_lo = (ik_all - ik_hi.astype(F32)).astype(BF16)
    ik3 = jnp.concatenate([ik_hi, ik_lo, ik_hi], axis=1)
    iq = iq_ref[0]
    iwt = iwt_ref[0]
    wide = tq % V7X_LANES == 0

    def iq_pieces(h):
        iqh = iq[:, h * IDX_DIM:(h + 1) * IDX_DIM]
        hi = iqh.astype(BF16)
        lo = (iqh - hi.astype(F32)).astype(BF16)
        return jnp.concatenate([hi, hi, lo], axis=1)

    if wide:
        logits = _dot_nt(ik3, jnp.concatenate([iq_pieces(h) for h in range(IDX_HEADS)], axis=0))
        logit = [logits[:, h * tq:(h + 1) * tq] for h in range(IDX_HEADS)]
    else:
        logit = [_dot_nt(ik3, iq_pieces(h)) for h in range(IDX_HEADS)]
    score = jnp.zeros((e, tq), F32)
    for h in range(IDX_HEADS):
        score = score + iwt[h:h + 1, :] * jnp.maximum(logit[h], 0.0)
    score_scr[...] = on_window(score, lambda s: jnp.where(adm_win, s, -jnp.inf))

    def key_to_f32(c):
        return lax.bitcast_convert_type(c ^ ((c >> 31) & 0x7FFFFFFF), F32)

    def count_ge(cand):
        cf = key_to_f32(cand)
        slab = 8 * V7X_SUBLANES
        if e % slab:
            return _col_reduce(jnp.where(score_scr[...] >= cf, 1.0, 0.0), jnp.sum)
        acc = jnp.zeros((slab, tq), F32)
        for i in range(e // slab):
            acc = jnp.where(score_scr[i * slab:(i + 1) * slab, :] >= cf, acc + 1.0, acc)
        return jnp.sum(acc, axis=0, keepdims=True)

    c0 = count_ge(jnp.zeros((1, tq), I32))
    thr = jnp.where(c0 >= topk, 0, _INT_MIN).astype(I32)
    cnt = jnp.where(c0 >= topk, c0, float(e))

    def thr_step(i, carry):
        thr, cnt = carry
        cand = thr | lax.shift_left(jnp.int32(1), 30 - i)
        c = count_ge(cand)
        ok = c >= topk
        return jnp.where(ok, cand, thr), jnp.where(ok, c, cnt)

    thr, cnt = lax.fori_loop(0, 31, thr_step, (thr, cnt))
    thr_f = jnp.where(thr == _INT_MIN, -jnp.inf, key_to_f32(thr))

    def store_bias(sel_bias):
        bias_scr[...] = on_window(sel_bias, lambda s: jnp.where(adm_win, s, _NEG))

    store_bias(jnp.where(score_scr[...] >= thr_f, 0.0, _NEG))

    @pl.when(jnp.max(cnt) > topk)
    def _():
        sc = score_scr[...]
        need = topk - _col_reduce(jnp.where(sc > thr_f, 1.0, 0.0), jnp.sum)
        blk = next(b for b in (256, 128, 64, 32, 16, 8) if e % b == 0)
        ri = lax.broadcasted_iota(I32, (blk, blk), 0)
        ci = lax.broadcasted_iota(I32, (blk, blk), 1)
        tri = jnp.where(ri >= ci, 1.0, 0.0).astype(BF16)
        carry = jnp.zeros((1, tq), F32)
        blocks = []
        for j in range(e // blk):
            scj = sc[j * blk:(j + 1) * blk]
            tied = scj == thr_f
            rank = _dot(tri, jnp.where(tied, 1.0, 0.0).astype(BF16)) + carry
            carry = rank[blk - 1:blk, :]
            blocks.append(jnp.where(scj > thr_f, 0.0, jnp.where(tied, jnp.where(rank <= need, 0.0, _NEG), _NEG)))
        store_bias(jnp.concatenate(blocks, axis=0))

    assert e <= _POS_SPLIT * _POS_SPLIT
    pr = lax.broadcasted_iota(I32, (e, ATT_HD), 0)
    pc = lax.broadcasted_iota(I32, (e, ATT_HD), 1)
    pos_lo = pr & (_POS_SPLIT - 1)
    pos_cols = jnp.where(pc < 2 * _SLOPE_PIECES, jnp.where((pc & 1) == 0, pr - pos_lo, pos_lo), 0)
    pos_cols = pos_cols.astype(F32).astype(BF16)
    k_aug = [jnp.concatenate([k_all[:, g * ATT_HD:(g + 1) * ATT_HD].astype(BF16), pos_cols], axis=1)
             for g in range(ATT_KV_HEADS)]
    ahead = -2.0 * jnp.maximum(wr - wc, 0).astype(F32)
    qc = lax.broadcasted_iota(I32, (tq, ATT_HD), 1)
    q = q_ref[0]

    def slope_of(h):
        return jnp.float32(_LOG2E * 2.0 ** (-8.0 * (h + 1) / ATT_HEADS))

    def q_aug(h):
        rest = jnp.full((tq, ATT_HD), slope_of(h), F32)
        cols = jnp.zeros((tq, ATT_HD), F32)
        for i in range(_SLOPE_PIECES):
            piece = rest.astype(BF16).astype(F32)
            rest = rest - piece
            cols = jnp.where((qc >> 1) == i, piece, cols)
        return jnp.concatenate([q[:, h * ATT_HD:(h + 1) * ATT_HD], cols.astype(BF16)], axis=1)

    def probs_of(s, h):
        s = on_window(s, lambda sw: sw + slope_of(h) * ahead) + bias_scr[...]
        return jnp.exp2(s - _col_reduce(s, jnp.max)).astype(BF16)

    ones_rows = jnp.ones((_BF16_SUBLANES, e), BF16)
    vt_aug = [jnp.concatenate([vt[g * ATT_HD:(g + 1) * ATT_HD, :], ones_rows], axis=0) for g in range(ATT_KV_HEADS)]

    def normalised(ot):
        return ot[:ATT_HD] / ot[ATT_HD:ATT_HD + 1]

    outs = []
    if wide:
        group_logits = [
            _dot_nt(k_aug[g], jnp.concatenate([q_aug(g * ATT_REP + r) for r in range(ATT_REP)], axis=0))
            for g in range(ATT_KV_HEADS)]
        for g in range(ATT_KV_HEADS):
            probs = [probs_of(group_logits[g][:, r * tq:(r + 1) * tq], g * ATT_REP + r) for r in range(ATT_REP)]
            ot = _dot(vt_aug[g], jnp.concatenate(probs, axis=1))
            outs += [normalised(ot[:, r * tq:(r + 1) * tq]) for r in range(ATT_REP)]
    else:
        ahead_heads = max(1, min(ATT_HEADS - 1, _DSA_LOGIT_ROWS_IN_FLIGHT // e))
        qk = lambda h: _dot_nt(k_aug[h // ATT_REP], q_aug(h))
        pending = [qk(h) for h in range(ahead_heads)]
        for h in range(ATT_HEADS):
            s = pending.pop(0)
            if h + ahead_heads < ATT_HEADS:
                pending.append(qk(h + ahead_heads))
            outs.append(normalised(_dot(vt_aug[h // ATT_REP], probs_of(s, h))))
    ot = jnp.concatenate(outs, axis=0)
    if tq % V7X_LANES:
        ot = jnp.concatenate([ot, jnp.zeros((ATT_Q, V7X_LANES - tq % V7X_LANES), F32)], axis=1)
    o_ref[0] = ot.T[:tq].astype(o_ref.dtype)


def _dsa_call(q_att, iq, iwt, k_new, vt_new, ik_new, past, *, tq, tile, topk):
    b = q_att.shape[0]
    p_len = 0 if past is None else past[0].shape[1]
    e_new = (tile + 1) * tq
    e = p_len + e_new
    in_specs, args = [], []
    if past is not None:
        in_specs += [pl.BlockSpec((1, p_len, w), lambda i: (i, 0, 0)) for w in (ATT_KV, ATT_KV, IDX_DIM)]
        args += list(past)
    in_specs += [
        pl.BlockSpec((1, e_new, ATT_KV), lambda i: (i, 0, 0)),
        pl.BlockSpec((1, ATT_KV, e_new), lambda i: (i, 0, 0)),
        pl.BlockSpec((1, e_new, IDX_DIM), lambda i: (i, 0, 0)),
        pl.BlockSpec((1, tq, ATT_Q), lambda i: (i, tile, 0)),
        pl.BlockSpec((1, tq, IDX_Q), lambda i: (i, tile, 0)),
        pl.BlockSpec((1, _IWT_ROWS, tq), lambda i: (i, 0, tile)),
    ]
    args += [k_new, vt_new, ik_new, q_att, iq, iwt]
    kern = functools.partial(_dsa_kernel, p_len=p_len, e_new=e_new, tq=tq, topk=topk)
    return pl.pallas_call(
        kern,
        out_shape=jax.ShapeDtypeStruct((b, tq, ATT_Q), BF16),
        grid=(b,),
        in_specs=in_specs,
        out_specs=pl.BlockSpec((1, tq, ATT_Q), lambda i: (i, 0, 0)),
        scratch_shapes=[
            pltpu.VMEM((e, tq), F32),
            pltpu.VMEM((e, tq), F32),
        ],
        compiler_params=_params("parallel"),
        name=f"dsa_e{e}",
    )(*args)


def _dsa(q_att, iq, iwt, k_new, vt_new, ik_new, past):
    b, l, _ = q_att.shape
    p_len = 0 if past is None else past[0].shape[1]
    topk = min(TOPK_MAX, (p_len + l) // 4)
    if l <= CHUNK:
        return _dsa_call(q_att, iq, iwt, k_new, vt_new, ik_new, past, tq=l, tile=0, topk=topk)
    assert l % _DSA_TQ == 0
    outs = [_dsa_call(q_att, iq, iwt, k_new, vt_new, ik_new, past, tq=_DSA_TQ, tile=t, topk=topk)
            for t in range(l // _DSA_TQ)]
    return jnp.concatenate(outs, axis=1)


def _block_diag_ones(n, blk):
    i = jnp.arange(n) // blk
    return (i[:, None] == i[None, :])


def _prep_weights(w_ada, b_ada, g_ffn1, w1_gate, w1_up, w1_down, g_mix, w_in, w_a2, b_a, g_gla,
                  g_q, g_k, w_out, g_ffn2, w2_gate, w2_up, w2_down, g_out):
    sizes = (GLA_QK, GLA_QK, GLA_V, GLA_LOWRANK, GLA_V, ATT_Q, ATT_KV, ATT_KV, IDX_Q, IDX_DIM, IDX_HEADS)
    pts, acc = [], 0
    for s in sizes[:-1]:
        acc += s
        pts.append(acc)
    gq, gk, gv, ga, gr, aq, ak, av, iq, ik, iw = jnp.split(w_in, pts, axis=1)
    pad = jnp.zeros((D_MODEL, _PACKED_WIDTH - sum(sizes)), w_in.dtype)
    w_packed = jnp.concatenate([gq, gk, gv, gr, aq, ak, av, iq, ik, ga, iw, pad], axis=1).astype(BF16)
    wa2 = jnp.zeros((V7X_LANES, GLA_QK), F32).at[_MISC_GA:_MISC_GA + GLA_LOWRANK].set(w_a2).astype(BF16)
    row = lambda a: a.reshape(1, -1).astype(F32)
    bd_state = jnp.repeat(jnp.repeat(jnp.eye(GLA_HEADS, dtype=F32), GLA_DK, axis=0), GLA_DV, axis=1)
    return dict(
        w_ada=w_ada.astype(BF16), b_ada=row(b_ada),
        g_ffn1=row(g_ffn1), w1=(w1_gate.astype(BF16), w1_up.astype(BF16), w1_down.astype(BF16)),
        g_mix=row(g_mix), w_packed=w_packed, wa2=wa2, b_a=row(b_a),
        g_gla=row(g_gla), gq_t=row(jnp.tile(g_q, ATT_HEADS)), gk_t=row(jnp.tile(g_k, ATT_KV_HEADS)),
        bd_heads=_block_diag_ones(ATT_Q, ATT_HD).astype(BF16), bd_state=bd_state,
        w_out=w_out.astype(BF16),
        g_ffn2=row(g_ffn2), w2=(w2_gate.astype(BF16), w2_up.astype(BF16), w2_down.astype(BF16)),
        g_out=row(g_out),
    )


def _layer(x, mod, past, s0, w):
    b, l, _ = x.shape
    x1 = _ffn(x, mod, w["g_ffn1"], *w["w1"], mod_base=0)
    (qk, gv, sgr, la, q_att, k_new, v_new, vt_new, ik_new, iq, iwt) = _proj(
        x1, mod, w["g_mix"], w["w_packed"], w["wa2"], w["b_a"], w["gq_t"], w["gk_t"], w["bd_heads"])
    gla_out, s_t = _gla(qk, gv, la, sgr, s0, w["g_gla"], w["bd_state"])
    att = _dsa(q_att, iq, iwt, k_new, vt_new, ik_new, past)
    y = _ffn(x1, mod, w["g_ffn2"], *w["w2"], mod_base=6, mix=(gla_out, att, w["w_out"]), g_out=w["g_out"])
    kv_shape = (b, l, ATT_KV_HEADS, ATT_HD)
    return y, k_new.reshape(kv_shape), v_new.reshape(kv_shape), ik_new, s_t


def kernel(x_prompt, x_sample, c_prompt, c_sample, cache_k, cache_v, cache_idx_k, state_gla, w_ada, b_ada, g_ffn1, w1_gate, w1_up, w1_down, g_mix, w_in, w_a2, b_a, g_gla, g_q, g_k, w_out, g_ffn2, w2_gate, w2_up, w2_down, g_out):
    depth = w_ada.shape[0]
    bp = x_prompt.shape[0]
    yp, ys = x_prompt, x_sample
    outs_p, outs_s = [], []
    for layer in range(depth):
        w = _prep_weights(*(t[layer] for t in (
            w_ada, b_ada, g_ffn1, w1_gate, w1_up, w1_down, g_mix, w_in, w_a2, b_a, g_gla,
            g_q, g_k, w_out, g_ffn2, w2_gate, w2_up, w2_down, g_out)))
        mod = _adaln(jnp.concatenate([c_prompt, c_sample], axis=0), w["w_ada"], w["b_ada"])
        mod = mod.reshape(mod.shape[0], 9, D_MODEL)
        ds, pp = cache_k.shape[1], cache_k.shape[2]
        past = (cache_k[layer].reshape(ds, pp, ATT_KV), cache_v[layer].reshape(ds, pp, ATT_KV), cache_idx_k[layer])
        yp, *rest_p = _layer(yp, mod[:bp], None, None, w)
        ys, *rest_s = _layer(ys, mod[bp:], past, state_gla[layer], w)
        outs_p.append(rest_p)
        outs_s.append(rest_s)
    stack = lambda outs, i: jnp.stack([o[i] for o in outs])
    return (yp, ys,
            stack(outs_p, 0), stack(outs_p, 1), stack(outs_p, 2), stack(outs_p, 3),
            stack(outs_s, 0), stack(outs_s, 1), stack(outs_s, 2), stack(outs_s, 3))
```

```python
import functools

import jax
import jax.numpy as jnp
from jax import lax
from jax.experimental import pallas as pl
from jax.experimental.pallas import tpu as pltpu

F32 = jnp.float32
BF16 = jnp.bfloat16
I32 = jnp.int32

D_MODEL = 1024
D_FF = 2816
CHUNK = 64
GLA_HEADS = 4
GLA_DK = 64
GLA_DV = 128
GLA_LOWRANK = 16
GLA_TAU = 16.0
GLA_BLOCK = 16
ATT_HEADS = 8
ATT_KV_HEADS = 2
ATT_HD = 64
IDX_HEADS = 4
IDX_DIM = 64
TOPK_MAX = 256
EPS = 1e-6

GLA_QK = GLA_HEADS * GLA_DK
GLA_V = GLA_HEADS * GLA_DV
ATT_Q = ATT_HEADS * ATT_HD
ATT_KV = ATT_KV_HEADS * ATT_HD
IDX_Q = IDX_HEADS * IDX_DIM
MIX_WIDTH = GLA_V + ATT_Q
ATT_REP = ATT_HEADS // ATT_KV_HEADS

V7X_LANES = 128
V7X_SUBLANES = 8
V7X_VMEM_LIMIT_BYTES = 56 * 1024 * 1024

_SEG_GQK = 0
_SEG_GV = 512
_SEG_GR = 1024
_SEG_AQ = 1536
_SEG_AKV = 2048
_SEG_IQ = 2304
_SEG_MISC = 2560
_PACKED_WIDTH = 2688
_MISC_GA = IDX_DIM
_MISC_IW = IDX_DIM + GLA_LOWRANK
_IWT_ROWS = V7X_SUBLANES

_FF_CHUNK = 256
_GLA_T = 64
_DSA_TQ = 256
_DSA_LOGIT_ROWS_IN_FLIGHT = 4096
_NEG = -1e30
_INT_MIN = -(2 ** 31)
_CHUNK_SHIFT = CHUNK.bit_length() - 1
_POS_SPLIT = 256
_SLOPE_PIECES = 3
_LOG2E = 1.4426950408889634
_BF16_SUBLANES = 2 * V7X_SUBLANES
assert 1 << _CHUNK_SHIFT == CHUNK

_NT = (((1,), (1,)), ((), ()))


def _dot(a, b):
    return jnp.dot(a, b, preferred_element_type=F32)


def _dot_nt(a, b):
    return lax.dot_general(a, b, _NT, preferred_element_type=F32)


def _rms(x):
    return x * lax.rsqrt(jnp.mean(x * x, axis=-1, keepdims=True) + EPS)


def _silu(x):
    return x * jax.nn.sigmoid(x)


def _col_reduce(x, op, chains=8):
    r, n = x.shape
    while chains > 1 and r % (chains * V7X_SUBLANES):
        chains //= 2
    if r % (chains * V7X_SUBLANES):
        return op(x, axis=0, keepdims=True)
    slab = chains * V7X_SUBLANES
    pair = jnp.add if op is jnp.sum else jnp.maximum
    acc = x[:slab]
    for i in range(1, r // slab):
        acc = pair(acc, x[i * slab:(i + 1) * slab])
    return op(acc, axis=0, keepdims=True)


def _row_tiling(batch, length, target):
    if length >= target:
        assert length % target == 0
        return 1, target
    nb = max(1, min(batch, target // length))
    while batch % nb:
        nb -= 1
    return nb, length


def _const_spec(shape):
    zeros = (0,) * len(shape)
    return pl.BlockSpec(shape, lambda *_: zeros, pipeline_mode=pl.Buffered(1))


def _params(*sem):
    return pltpu.CompilerParams(dimension_semantics=sem, vmem_limit_bytes=V7X_VMEM_LIMIT_BYTES)


def _adaln_kernel(c_ref, w_ref, b_ref, o_ref):
    a = _silu(c_ref[...]).astype(BF16)
    o_ref[...] = _dot(a, w_ref[...]) + b_ref[...]


def _adaln(c, w_ada, b_ada):
    bt, d = c.shape
    n = w_ada.shape[1]
    tn = d
    return pl.pallas_call(
        _adaln_kernel,
        out_shape=jax.ShapeDtypeStruct((bt, n), F32),
        grid=(n // tn,),
        in_specs=[
            pl.BlockSpec((bt, d), lambda j: (0, 0)),
            pl.BlockSpec((d, tn), lambda j: (0, j)),
            pl.BlockSpec((1, tn), lambda j: (0, j)),
        ],
        out_specs=pl.BlockSpec((bt, tn), lambda j: (0, j)),
        compiler_params=_params("arbitrary"),
        name="adaln",
    )(c, w_ada, b_ada)


def _ffn_kernel(*refs, nb, rows, mod_base, with_mix, with_final_norm):
    it = iter(refs)
    x_ref, mod_ref = next(it), next(it)
    if with_mix:
        gla_ref, att_ref, wout_ref = next(it), next(it), next(it)
    g_ref, wg_ref, wu_ref, wd_ref = next(it), next(it), next(it), next(it)
    if with_final_norm:
        gout_ref = next(it)
    o_ref, h_scr, x_scr = next(it), next(it), next(it)
    tm = nb * rows

    x = x_ref[...].reshape(tm, D_MODEL)
    if with_mix:
        gla = gla_ref[...].reshape(tm, GLA_V)
        att = att_ref[...].reshape(tm, ATT_Q)
        mix = _dot(gla, wout_ref[:GLA_V, :]) + _dot(att, wout_ref[GLA_V:, :])
    g = g_ref[...]
    for j in range(nb):
        sl = slice(j * rows, (j + 1) * rows)
        m = mod_ref[j]
        xj = x[sl]
        if with_mix:
            xj = xj + m[5:6] * mix[sl]
        x_scr[sl, :] = xj
        h = (_rms(xj) * g) * (1.0 + m[mod_base + 1:mod_base + 2]) + m[mod_base:mod_base + 1]
        h_scr[sl, :] = h.astype(BF16)

    h = h_scr[...]
    acc = jnp.zeros((tm, D_MODEL), F32)
    for c in range(D_FF // _FF_CHUNK):
        cs = slice(c * _FF_CHUNK, (c + 1) * _FF_CHUNK)
        gate = _dot(h, wg_ref[:, cs])
        up = _dot(h, wu_ref[:, cs])
        a = (_silu(gate) * up).astype(BF16)
        acc = acc + _dot(a, wd_ref[cs, :])

    for j in range(nb):
        sl = slice(j * rows, (j + 1) * rows)
        m = mod_ref[j]
        y = x_scr[sl, :] + 0.5 * m[mod_base + 2:mod_base + 3] * acc[sl]
        if with_final_norm:
            y = _rms(y) * gout_ref[...]
        o_ref[j] = y


def _ffn(x, mod, g, wg, wu, wd, *, mod_base, mix=None, g_out=None, tm=512):
    b, l, d = x.shape
    nb, rows = _row_tiling(b, l, tm)
    grid = (b // nb, l // rows)
    row_spec = lambda w: pl.BlockSpec((nb, rows, w), lambda i, r: (i, r, 0))
    in_specs = [row_spec(d), pl.BlockSpec((nb, 9, d), lambda i, r: (i, 0, 0))]
    args = [x, mod]
    if mix is not None:
        gla, att, wout = mix
        in_specs += [row_spec(GLA_V), row_spec(ATT_Q), _const_spec(wout.shape)]
        args += [gla, att, wout]
    in_specs += [_const_spec(g.shape), _const_spec(wg.shape), _const_spec(wu.shape), _const_spec(wd.shape)]
    args += [g, wg, wu, wd]
    if g_out is not None:
        in_specs.append(_const_spec(g_out.shape))
        args.append(g_out)
    kern = functools.partial(
        _ffn_kernel, nb=nb, rows=rows, mod_base=mod_base,
        with_mix=mix is not None, with_final_norm=g_out is not None)
    return pl.pallas_call(
        kern,
        out_shape=jax.ShapeDtypeStruct((b, l, d), F32),
        grid=grid,
        in_specs=in_specs,
        out_specs=row_spec(d),
        scratch_shapes=[pltpu.VMEM((nb * rows, d), BF16), pltpu.VMEM((nb * rows, d), F32)],
        compiler_params=_params("parallel", "parallel"),
        name="ffn_mix" if mix is not None else "ffn",
    )(*args)


def _proj_kernel(x_ref, mod_ref, g_ref, w_ref, wa2_ref, ba_ref, gq_ref, gk_ref, bd_ref,
                 qk_ref, v_ref, sgr_ref, la_ref, qatt_ref, knew_ref, vnew_ref, vtnew_ref,
                 iknew_ref, iq_ref, iwt_ref, h_scr, *, nb, rows):
    tm = nb * rows
    x = x_ref[...].reshape(tm, D_MODEL)
    g = g_ref[...]
    for j in range(nb):
        sl = slice(j * rows, (j + 1) * rows)
        m = mod_ref[j]
        h = (_rms(x[sl]) * g) * (1.0 + m[4:5]) + m[3:4]
        h_scr[sl, :] = h.astype(BF16)
    h = h_scr[...]

    def seg(start, width):
        return _dot(h, w_ref[:, start:start + width])

    def put(ref, val, width):
        ref[...] = val.reshape(nb, rows, width).astype(ref.dtype)

    lane = lax.broadcasted_iota(I32, (1, 2 * GLA_QK), 1)
    qscale = jnp.where(lane < GLA_QK, GLA_DK ** -0.5, 1.0).astype(F32)
    put(qk_ref, seg(_SEG_GQK, 2 * GLA_QK) * qscale, 2 * GLA_QK)
    put(v_ref, seg(_SEG_GV, GLA_V), GLA_V)
    put(sgr_ref, _silu(seg(_SEG_GR, GLA_V)), GLA_V)

    aq = seg(_SEG_AQ, ATT_Q)
    msq = _dot((aq * aq).astype(BF16), bd_ref[...]) * (1.0 / ATT_HD)
    put(qatt_ref, aq * lax.rsqrt(msq + EPS) * gq_ref[...] * (ATT_HD ** -0.5 * _LOG2E), ATT_Q)

    akv = seg(_SEG_AKV, 2 * ATT_KV)
    ak, av = akv[:, :ATT_KV], akv[:, ATT_KV:]
    msk = _dot((ak * ak).astype(BF16), bd_ref[:ATT_KV, :ATT_KV]) * (1.0 / ATT_HD)
    put(knew_ref, ak * lax.rsqrt(msk + EPS) * gk_ref[...], ATT_KV)
    put(vnew_ref, av, ATT_KV)
    avt = av.T
    for j in range(nb):
        vtnew_ref[j] = avt[:, j * rows:(j + 1) * rows].astype(BF16)

    put(iq_ref, seg(_SEG_IQ, IDX_Q), IDX_Q)

    misc = seg(_SEG_MISC, V7X_LANES)
    put(iknew_ref, misc[:, :IDX_DIM], IDX_DIM)
    za = _dot(misc.astype(BF16), wa2_ref[...]) + ba_ref[...]
    log_sig = jnp.minimum(za, 0.0) - jnp.log(1.0 + jnp.exp(-jnp.abs(za)))
    put(la_ref, log_sig * (1.0 / GLA_TAU), GLA_QK)
    misct = misc.T
    iwt = misct[_MISC_IW:_MISC_IW + _IWT_ROWS, :] * ((IDX_HEADS * IDX_DIM) ** -0.5)
    for j in range(nb):
        iwt_ref[j] = iwt[:, j * rows:(j + 1) * rows]


def _proj(x, mod, g_mix, w_packed, wa2, b_a, gq_t, gk_t, bd, *, tm=512):
    b, l, d = x.shape
    nb, rows = _row_tiling(b, l, tm)
    grid = (b // nb, l // rows)
    row_spec = lambda w: pl.BlockSpec((nb, rows, w), lambda i, r: (i, r, 0))
    col_spec = lambda h: pl.BlockSpec((nb, h, rows), lambda i, r: (i, 0, r))
    sds = lambda w, dt: jax.ShapeDtypeStruct((b, l, w), dt)
    out_shape = [
        sds(2 * GLA_QK, F32), sds(GLA_V, BF16), sds(GLA_V, BF16), sds(GLA_QK, F32), sds(ATT_Q, BF16),
        sds(ATT_KV, F32), sds(ATT_KV, F32), jax.ShapeDtypeStruct((b, ATT_KV, l), BF16),
        sds(IDX_DIM, F32), sds(IDX_Q, F32), jax.ShapeDtypeStruct((b, _IWT_ROWS, l), F32),
    ]
    out_specs = [
        row_spec(2 * GLA_QK), row_spec(GLA_V), row_spec(GLA_V), row_spec(GLA_QK), row_spec(ATT_Q),
        row_spec(ATT_KV), row_spec(ATT_KV), col_spec(ATT_KV),
        row_spec(IDX_DIM), row_spec(IDX_Q), col_spec(_IWT_ROWS),
    ]
    consts = [g_mix, w_packed, wa2, b_a, gq_t, gk_t, bd]
    return pl.pallas_call(
        functools.partial(_proj_kernel, nb=nb, rows=rows),
        out_shape=out_shape,
        grid=grid,
        in_specs=[row_spec(d), pl.BlockSpec((nb, 9, d), lambda i, r: (i, 0, 0))]
        + [_const_spec(c.shape) for c in consts],
        out_specs=out_specs,
        scratch_shapes=[pltpu.VMEM((nb * rows, d), BF16)],
        compiler_params=_params("parallel", "parallel"),
        name="proj",
    )(x, mod, *consts)


def _gla_kernel(*refs, has_s0, nbb):
    it = iter(refs)
    qk_ref, v_ref, la_ref, sgr_ref = next(it), next(it), next(it), next(it)
    s0_ref = next(it) if has_s0 else None
    g_ref, bd_ref, o_ref, st_ref, s_scr = next(it), next(it), next(it), next(it), next(it)
    t, cb = _GLA_T, GLA_BLOCK
    nsb = t // cb
    c = pl.program_id(1)

    @pl.when(c == 0)
    def _():
        s_scr[...] = jnp.zeros((nbb, GLA_QK, GLA_V), F32)
        if has_s0:
            for j in range(nbb):
                for h in range(GLA_HEADS):
                    s_scr[j, h * GLA_DK:(h + 1) * GLA_DK, h * GLA_DV:(h + 1) * GLA_DV] = s0_ref[j, h]

    ri = lax.broadcasted_iota(I32, (t, t), 0)
    ci = lax.broadcasted_iota(I32, (t, t), 1)
    tril = jnp.where(ri >= ci, 1.0, 0.0).astype(BF16)
    lane = lax.broadcasted_iota(I32, (1, GLA_QK), 1)
    head_mask = [jnp.where((lane >= h * GLA_DK) & (lane < (h + 1) * GLA_DK), 1.0, 0.0).astype(F32)
                 for h in range(GLA_HEADS)]
    g = g_ref[...]


    cum = []
    for j in range(nbb):
        la = la_ref[j]
        p0 = la.astype(BF16)
        r0 = la - p0.astype(F32)
        p1 = r0.astype(BF16)
        p2 = (r0 - p1.astype(F32)).astype(BF16)
        bb = _dot(tril, jnp.concatenate([p0, p1, p2], axis=1))
        cum.append(bb[:, :GLA_QK] + bb[:, GLA_QK:2 * GLA_QK] + bb[:, 2 * GLA_QK:])

    o_inter, scores = [], []
    for j in range(nbb):
        b = cum[j]
        qk = qk_ref[j]
        q, k = qk[:, :GLA_QK], qk[:, GLA_QK:]
        v = v_ref[j]
        b0 = [jnp.zeros((1, GLA_QK), F32)] + [b[cb * i - 1:cb * i, :] for i in range(1, nsb)]
        btot = b[t - 1:t, :]
        bstart = jnp.concatenate([jnp.broadcast_to(b0[i], (cb, GLA_QK)) for i in range(nsb)], axis=0)
        q_rel = q * jnp.exp(b - bstart)
        q_int = (q * jnp.exp(b)).astype(BF16)
        k_end = k * jnp.exp(btot - b)

        s_old = s_scr[j]
        o_inter.append(_dot(q_int, s_old.astype(BF16)))
        ds = _dot(k_end.T.astype(BF16), v)
        dcol = jnp.exp(jnp.broadcast_to(btot, (V7X_SUBLANES, GLA_QK)).T[:, 0:1])
        s_scr[j] = s_old * dcol + ds * bd_ref[...]

        sc = []
        for i in range(nsb):
            n = cb * (i + 1)
            qs = q_rel[cb * i:cb * (i + 1)]
            q_stack = jnp.concatenate([qs * head_mask[h] for h in range(GLA_HEADS)], axis=0).astype(BF16)
            km = (k[:n] * jnp.exp(b0[i] - b[:n])).astype(BF16)
            sc.append(_dot_nt(q_stack, km))
        scores.append(sc)

    for j in range(nbb):
        v = v_ref[j]
        intra = []
        for i in range(nsb):
            n = cb * (i + 1)
            rr = lax.broadcasted_iota(I32, (GLA_HEADS * cb, n), 0)
            cc = lax.broadcasted_iota(I32, (GLA_HEADS * cb, n), 1)
            a = jnp.where(cc - cb * i <= (rr & (cb - 1)), scores[j][i], 0.0).astype(BF16)
            oi = _dot(a, v[:n])
            intra.append(jnp.concatenate(
                [oi[h * cb:(h + 1) * cb, h * GLA_DV:(h + 1) * GLA_DV] for h in range(GLA_HEADS)], axis=1))
        o = o_inter[j] + jnp.concatenate(intra, axis=0)
        on = jnp.concatenate([_rms(o[:, h * GLA_DV:(h + 1) * GLA_DV]) * g for h in range(GLA_HEADS)], axis=1)
        o_ref[j] = (on * sgr_ref[j].astype(F32)).astype(o_ref.dtype)

    @pl.when(c == pl.num_programs(1) - 1)
    def _():
        for j in range(nbb):
            for h in range(GLA_HEADS):
                st_ref[j, h] = s_scr[j, h * GLA_DK:(h + 1) * GLA_DK, h * GLA_DV:(h + 1) * GLA_DV]


def _gla(qk, v, la, sgr, s0, g_gla, bd_state, *, nbb=8):
    b, l, _ = qk.shape
    assert l % _GLA_T == 0
    while b % nbb:
        nbb -= 1
    tok = lambda w: pl.BlockSpec((nbb, _GLA_T, w), lambda i, c: (i, c, 0))
    st_spec = pl.BlockSpec((nbb, GLA_HEADS, GLA_DK, GLA_DV), lambda i, c: (i, 0, 0, 0))
    in_specs = [tok(2 * GLA_QK), tok(GLA_V), tok(GLA_QK), tok(GLA_V)]
    args = [qk, v, la, sgr]
    if s0 is not None:
        in_specs.append(st_spec)
        args.append(s0)
    in_specs += [_const_spec(g_gla.shape), _const_spec(bd_state.shape)]
    args += [g_gla, bd_state]
    return pl.pallas_call(
        functools.partial(_gla_kernel, has_s0=s0 is not None, nbb=nbb),
        out_shape=[jax.ShapeDtypeStruct((b, l, GLA_V), BF16),
                   jax.ShapeDtypeStruct((b, GLA_HEADS, GLA_DK, GLA_DV), F32)],
        grid=(b // nbb, l // _GLA_T),
        in_specs=in_specs,
        out_specs=[tok(GLA_V), st_spec],
        scratch_shapes=[pltpu.VMEM((nbb, GLA_QK, GLA_V), F32)],
        compiler_params=_params("parallel", "arbitrary"),
        name="gla",
    )(*args)


def _dsa_kernel(*refs, p_len, e_new, tq, topk):
    has_past = p_len > 0
    it = iter(refs)
    if has_past:
        kp_ref, vp_ref, ikp_ref = next(it), next(it), next(it)
    kn_ref, vtn_ref, ikn_ref, q_ref, iq_ref, iwt_ref = (next(it) for _ in range(6))
    o_ref, score_scr, bias_scr = (next(it) for _ in range(3))
    e = p_len + e_new
    w0 = e - tq

    k_all, ik_all, vt = kn_ref[0], ikn_ref[0], vtn_ref[0]
    if has_past:
        k_all = jnp.concatenate([kp_ref[0], k_all], axis=0)
        ik_all = jnp.concatenate([ikp_ref[0], ik_all], axis=0)
        vt = jnp.concatenate([vp_ref[0].T.astype(BF16), vt], axis=1)

    def on_window(full, fn):
        return jnp.concatenate([full[:w0], fn(full[w0:])], axis=0) if w0 else fn(full)

    wr = lax.broadcasted_iota(I32, (tq, tq), 0)
    wc = lax.broadcasted_iota(I32, (tq, tq), 1)
    limit = (((w0 + wc) >> _CHUNK_SHIFT) + 1) << _CHUNK_SHIFT
    adm_win = (w0 + wr) < limit

    ik_hi = ik_all.astype(BF16)
    ik_lo = (ik_all - ik_hi.astype(F32)).astype(BF16)
    ik3 = jnp.concatenate([ik_hi, ik_lo, ik_hi], axis=1)
    iq = iq_ref[0]
    iwt = iwt_ref[0]
    wide = tq % V7X_LANES == 0

    def iq_pieces(h):
        iqh = iq[:, h * IDX_DIM:(h + 1) * IDX_DIM]
        hi = iqh.astype(BF16)
        lo = (iqh - hi.astype(F32)).astype(BF16)
        return jnp.concatenate([hi, hi, lo], axis=1)

    if wide:
        logits = _dot_nt(ik3, jnp.concatenate([iq_pieces(h) for h in range(IDX_HEADS)], axis=0))
        logit = [logits[:, h * tq:(h + 1) * tq] for h in range(IDX_HEADS)]
    else:
        logit = [_dot_nt(ik3, iq_pieces(h)) for h in range(IDX_HEADS)]
    score = jnp.zeros((e, tq), F32)
    for h in range(IDX_HEADS):
        score = score + iwt[h:h + 1, :] * jnp.maximum(logit[h], 0.0)
    score_scr[...] = on_window(score, lambda s: jnp.where(adm_win, s, -jnp.inf))

    def key_to_f32(c):
        return lax.bitcast_convert_type(c ^ ((c >> 31) & 0x7FFFFFFF), F32)

    def count_ge(cand):
        cf = key_to_f32(cand)
        slab = 8 * V7X_SUBLANES
        if e % slab:
            return _col_reduce(jnp.where(score_scr[...] >= cf, 1.0, 0.0), jnp.sum)
        acc = jnp.zeros((slab, tq), F32)
        for i in range(e // slab):
            acc = jnp.where(score_scr[i * slab:(i + 1) * slab, :] >= cf, acc + 1.0, acc)
        return jnp.sum(acc, axis=0, keepdims=True)

    c0 = count_ge(jnp.zeros((1, tq), I32))
    thr = jnp.where(c0 >= topk, 0, _INT_MIN).astype(I32)
    cnt = jnp.where(c0 >= topk, c0, float(e))

    def thr_step(i, carry):
        thr, cnt = carry
        cand = thr | lax.shift_left(jnp.int32(1), 30 - i)
        c = count_ge(cand)
        ok = c >= topk
        return jnp.where(ok, cand, thr), jnp.where(ok, c, cnt)

    thr, cnt = lax.fori_loop(0, 31, thr_step, (thr, cnt))
    thr_f = jnp.where(thr == _INT_MIN, -jnp.inf, key_to_f32(thr))

    sc = score_scr[...]
    blk = next(b for b in (256, 128, 64, 32, 16, 8) if e % b == 0)
    ri = lax.broadcasted_iota(I32, (blk, blk), 0)
    ci = lax.broadcasted_iota(I32, (blk, blk), 1)
    tri = jnp.where(ri >= ci, 1.0, 0.0).astype(BF16)
    n_tied = jnp.zeros((1, tq), F32)
    ranks = []
    for j in range(e // blk):
        tied = jnp.where(sc[j * blk:(j + 1) * blk] == thr_f, 1.0, 0.0).astype(BF16)
        ranks.append(_dot(tri, tied) + n_tied)
        n_tied = ranks[-1][blk - 1:blk, :]
    need = topk - (cnt - n_tied)
    sel_bias = jnp.concatenate([
        jnp.where(sc[j * blk:(j + 1) * blk] > thr_f, 0.0,
                  jnp.where(sc[j * blk:(j + 1) * blk] == thr_f, jnp.where(ranks[j] <= need, 0.0, _NEG), _NEG))
        for j in range(e // blk)], axis=0)
    bias_scr[...] = on_window(sel_bias, lambda s: jnp.where(adm_win, s, _NEG))

    assert e <= _POS_SPLIT * _POS_SPLIT
    pr = lax.broadcasted_iota(I32, (e, ATT_HD), 0)
    pc = lax.broadcasted_iota(I32, (e, ATT_HD), 1)
    pos_lo = pr & (_POS_SPLIT - 1)
    pos_cols = jnp.where(pc < 2 * _SLOPE_PIECES, jnp.where((pc & 1) == 0, pr - pos_lo, pos_lo), 0)
    pos_cols = pos_cols.astype(F32).astype(BF16)
    k_aug = [jnp.concatenate([k_all[:, g * ATT_HD:(g + 1) * ATT_HD].astype(BF16), pos_cols], axis=1)
             for g in range(ATT_KV_HEADS)]
    ahead = -2.0 * jnp.maximum(wr - wc, 0).astype(F32)
    qc = lax.broadcasted_iota(I32, (tq, ATT_HD), 1)
    q = q_ref[0]

    def slope_of(h):
        return jnp.float32(_LOG2E * 2.0 ** (-8.0 * (h + 1) / ATT_HEADS))

    def q_aug(h):
        rest = jnp.full((tq, ATT_HD), slope_of(h), F32)
        cols = jnp.zeros((tq, ATT_HD), F32)
        for i in range(_SLOPE_PIECES):
            piece = rest.astype(BF16).astype(F32)
            rest = rest - piece
            cols = jnp.where((qc >> 1) == i, piece, cols)
        return jnp.concatenate([q[:, h * ATT_HD:(h + 1) * ATT_HD], cols.astype(BF16)], axis=1)

    def probs_of(s, h):
        s = on_window(s, lambda sw: sw + slope_of(h) * ahead) + bias_scr[...]
        return jnp.exp2(s - _col_reduce(s, jnp.max)).astype(BF16)

    ones_rows = jnp.ones((_BF16_SUBLANES, e), BF16)
    vt_aug = [jnp.concatenate([vt[g * ATT_HD:(g + 1) * ATT_HD, :], ones_rows], axis=0) for g in range(ATT_KV_HEADS)]

    def normalised(ot):
        return ot[:ATT_HD] / ot[ATT_HD:ATT_HD + 1]

    outs = []
    if wide:
        group_logits = [
            _dot_nt(k_aug[g], jnp.concatenate([q_aug(g * ATT_REP + r) for r in range(ATT_REP)], axis=0))
            for g in range(ATT_KV_HEADS)]
        for g in range(ATT_KV_HEADS):
            probs = [probs_of(group_logits[g][:, r * tq:(r + 1) * tq], g * ATT_REP + r) for r in range(ATT_REP)]
            ot = _dot(vt_aug[g], jnp.concatenate(probs, axis=1))
            outs += [normalised(ot[:, r * tq:(r + 1) * tq]) for r in range(ATT_REP)]
    else:
        ahead_heads = max(1, min(ATT_HEADS - 1, _DSA_LOGIT_ROWS_IN_FLIGHT // e))
        qk = lambda h: _dot_nt(k_aug[h // ATT_REP], q_aug(h))
        pending = [qk(h) for h in range(ahead_heads)]
        for h in range(ATT_HEADS):
            s = pending.pop(0)
            if h + ahead_heads < ATT_HEADS:
                pending.append(qk(h + ahead_heads))
            outs.append(normalised(_dot(vt_aug[h // ATT_REP], probs_of(s, h))))
    ot = jnp.concatenate(outs, axis=0)
    if tq % V7X_LANES:
        ot = jnp.concatenate([ot, jnp.zeros((ATT_Q, V7X_LANES - tq % V7X_LANES), F32)], axis=1)
    o_ref[0] = ot.T[:tq].astype(o_ref.dtype)


def _dsa_call(q_att, iq, iwt, k_new, vt_new, ik_new, past, *, tq, tile, topk):
    b = q_att.shape[0]
    p_len = 0 if past is None else past[0].shape[1]
    e_new = (tile + 1) * tq
    e = p_len + e_new
    in_specs, args = [], []
    if past is not None:
        in_specs += [pl.BlockSpec((1, p_len, w), lambda i: (i, 0, 0)) for w in (ATT_KV, ATT_KV, IDX_DIM)]
        args += list(past)
    in_specs += [
        pl.BlockSpec((1, e_new, ATT_KV), lambda i: (i, 0, 0)),
        pl.BlockSpec((1, ATT_KV, e_new), lambda i: (i, 0, 0)),
        pl.BlockSpec((1, e_new, IDX_DIM), lambda i: (i, 0, 0)),
        pl.BlockSpec((1, tq, ATT_Q), lambda i: (i, tile, 0)),
        pl.BlockSpec((1, tq, IDX_Q), lambda i: (i, tile, 0)),
        pl.BlockSpec((1, _IWT_ROWS, tq), lambda i: (i, 0, tile)),
    ]
    args += [k_new, vt_new, ik_new, q_att, iq, iwt]
    kern = functools.partial(_dsa_kernel, p_len=p_len, e_new=e_new, tq=tq, topk=topk)
    return pl.pallas_call(
        kern,
        out_shape=jax.ShapeDtypeStruct((b, tq, ATT_Q), BF16),
        grid=(b,),
        in_specs=in_specs,
        out_specs=pl.BlockSpec((1, tq, ATT_Q), lambda i: (i, 0, 0)),
        scratch_shapes=[
            pltpu.VMEM((e, tq), F32),
            pltpu.VMEM((e, tq), F32),
        ],
        compiler_params=_params("parallel"),
        name=f"dsa_e{e}",
    )(*args)


def _dsa(q_att, iq, iwt, k_new, vt_new, ik_new, past):
    b, l, _ = q_att.shape
    p_len = 0 if past is None else past[0].shape[1]
    topk = min(TOPK_MAX, (p_len + l) // 4)
    if l <= CHUNK:
        return _dsa_call(q_att, iq, iwt, k_new, vt_new, ik_new, past, tq=l, tile=0, topk=topk)
    assert l % _DSA_TQ == 0
    outs = [_dsa_call(q_att, iq, iwt, k_new, vt_new, ik_new, past, tq=_DSA_TQ, tile=t, topk=topk)
            for t in range(l // _DSA_TQ)]
    return jnp.concatenate(outs, axis=1)


def _block_diag_ones(n, blk):
    i = jnp.arange(n) // blk
    return (i[:, None] == i[None, :])


def _prep_weights(w_ada, b_ada, g_ffn1, w1_gate, w1_up, w1_down, g_mix, w_in, w_a2, b_a, g_gla,
                  g_q, g_k, w_out, g_ffn2, w2_gate, w2_up, w2_down, g_out):
    sizes = (GLA_QK, GLA_QK, GLA_V, GLA_LOWRANK, GLA_V, ATT_Q, ATT_KV, ATT_KV, IDX_Q, IDX_DIM, IDX_HEADS)
    pts, acc = [], 0
    for s in sizes[:-1]:
        acc += s
        pts.append(acc)
    gq, gk, gv, ga, gr, aq, ak, av, iq, ik, iw = jnp.split(w_in, pts, axis=1)
    pad = jnp.zeros((D_MODEL, _PACKED_WIDTH - sum(sizes)), w_in.dtype)
    w_packed = jnp.concatenate([gq, gk, gv, gr, aq, ak, av, iq, ik, ga, iw, pad], axis=1).astype(BF16)
    wa2 = jnp.zeros((V7X_LANES, GLA_QK), F32).at[_MISC_GA:_MISC_GA + GLA_LOWRANK].set(w_a2).astype(BF16)
    row = lambda a: a.reshape(1, -1).astype(F32)
    bd_state = jnp.repeat(jnp.repeat(jnp.eye(GLA_HEADS, dtype=F32), GLA_DK, axis=0), GLA_DV, axis=1)
    return dict(
        w_ada=w_ada.astype(BF16), b_ada=row(b_ada),
        g_ffn1=row(g_ffn1), w1=(w1_gate.astype(BF16), w1_up.astype(BF16), w1_down.astype(BF16)),
        g_mix=row(g_mix), w_packed=w_packed, wa2=wa2, b_a=row(b_a),
        g_gla=row(g_gla), gq_t=row(jnp.tile(g_q, ATT_HEADS)), gk_t=row(jnp.tile(g_k, ATT_KV_HEADS)),
        bd_heads=_block_diag_ones(ATT_Q, ATT_HD).astype(BF16), bd_state=bd_state,
        w_out=w_out.astype(BF16),
        g_ffn2=row(g_ffn2), w2=(w2_gate.astype(BF16), w2_up.astype(BF16), w2_down.astype(BF16)),
        g_out=row(g_out),
    )


def _layer(x, mod, past, s0, w):
    b, l, _ = x.shape
    x1 = _ffn(x, mod, w["g_ffn1"], *w["w1"], mod_base=0)
    (qk, gv, sgr, la, q_att, k_new, v_new, vt_new, ik_new, iq, iwt) = _proj(
        x1, mod, w["g_mix"], w["w_packed"], w["wa2"], w["b_a"], w["gq_t"], w["gk_t"], w["bd_heads"])
    gla_out, s_t = _gla(qk, gv, la, sgr, s0, w["g_gla"], w["bd_state"])
    att = _dsa(q_att, iq, iwt, k_new, vt_new, ik_new, past)
    y = _ffn(x1, mod, w["g_ffn2"], *w["w2"], mod_base=6, mix=(gla_out, att, w["w_out"]), g_out=w["g_out"])
    kv_shape = (b, l, ATT_KV_HEADS, ATT_HD)
    return y, k_new.reshape(kv_shape), v_new.reshape(kv_shape), ik_new, s_t


def kernel(x_prompt, x_sample, c_prompt, c_sample, cache_k, cache_v, cache_idx_k, state_gla, w_ada, b_ada, g_ffn1, w1_gate, w1_up, w1_down, g_mix, w_in, w_a2, b_a, g_gla, g_q, g_k, w_out, g_ffn2, w2_gate, w2_up, w2_down, g_out):
    depth = w_ada.shape[0]
    bp = x_prompt.shape[0]
    yp, ys = x_prompt, x_sample
    outs_p, outs_s = [], []
    for layer in range(depth):
        w = _prep_weights(*(t[layer] for t in (
            w_ada, b_ada, g_ffn1, w1_gate, w1_up, w1_down, g_mix, w_in, w_a2, b_a, g_gla,
            g_q, g_k, w_out, g_ffn2, w2_gate, w2_up, w2_down, g_out)))
        mod = _adaln(jnp.concatenate([c_prompt, c_sample], axis=0), w["w_ada"], w["b_ada"])
        mod = mod.reshape(mod.shape[0], 9, D_MODEL)
        ds, pp = cache_k.shape[1], cache_k.shape[2]
        past = (cache_k[layer].reshape(ds, pp, ATT_KV), cache_v[layer].reshape(ds, pp, ATT_KV), cache_idx_k[layer])
        yp, *rest_p = _layer(yp, mod[:bp], None, None, w)
        ys, *rest_s = _layer(ys, mod[bp:], past, state_gla[layer], w)
        outs_p.append(rest_p)
        outs_s.append(rest_s)
    stack = lambda outs, i: jnp.stack([o[i] for o in outs])
    return (yp, ys,
            stack(outs_p, 0), stack(outs_p, 1), stack(outs_p, 2), stack(outs_p, 3),
            stack(outs_s, 0), stack(outs_s, 1), stack(outs_s, 2), stack(outs_s, 3))
```

```python
import functools

import jax
import jax.numpy as jnp
from jax import lax
from jax.experimental import pallas as pl
from jax.experimental.pallas import tpu as pltpu

F32 = jnp.float32
BF16 = jnp.bfloat16
I32 = jnp.int32

D_MODEL = 1024
D_FF = 2816
CHUNK = 64
GLA_HEADS = 4
GLA_DK = 64
GLA_DV = 128
GLA_LOWRANK = 16
GLA_TAU = 16.0
GLA_BLOCK = 16
ATT_HEADS = 8
ATT_KV_HEADS = 2
ATT_HD = 64
IDX_HEADS = 4
IDX_DIM = 64
TOPK_MAX = 256
EPS = 1e-6

GLA_QK = GLA_HEADS * GLA_DK
GLA_V = GLA_HEADS * GLA_DV
ATT_Q = ATT_HEADS * ATT_HD
ATT_KV = ATT_KV_HEADS * ATT_HD
IDX_Q = IDX_HEADS * IDX_DIM
MIX_WIDTH = GLA_V + ATT_Q
ATT_REP = ATT_HEADS // ATT_KV_HEADS

V7X_LANES = 128
V7X_SUBLANES = 8
V7X_VMEM_LIMIT_BYTES = 56 * 1024 * 1024

_SEG_GQK = 0
_SEG_GV = 512
_SEG_GR = 1024
_SEG_AQ = 1536
_SEG_AKV = 2048
_SEG_IQ = 2304
_SEG_MISC = 2560
_PACKED_WIDTH = 2688
_MISC_GA = IDX_DIM
_MISC_IW = IDX_DIM + GLA_LOWRANK
_IWT_ROWS = V7X_SUBLANES

_FF_CHUNK = 256
_GLA_T = 64
_DSA_TQ = 256
_DSA_LOGIT_ROWS_IN_FLIGHT = 4096
_NEG = -1e30
_INT_MIN = -(2 ** 31)
_CHUNK_SHIFT = CHUNK.bit_length() - 1
_POS_SPLIT = 256
_SLOPE_PIECES = 3
_LOG2E = 1.4426950408889634
_BF16_SUBLANES = 2 * V7X_SUBLANES
assert 1 << _CHUNK_SHIFT == CHUNK

_NT = (((1,), (1,)), ((), ()))


def _dot(a, b):
    return jnp.dot(a, b, preferred_element_type=F32)


def _dot_nt(a, b):
    return lax.dot_general(a, b, _NT, preferred_element_type=F32)


def _rms(x):
    return x * lax.rsqrt(jnp.mean(x * x, axis=-1, keepdims=True) + EPS)


def _silu(x):
    return x * jax.nn.sigmoid(x)


def _col_reduce(x, op, chains=8):
    r, n = x.shape
    while chains > 1 and r % (chains * V7X_SUBLANES):
        chains //= 2
    if r % (chains * V7X_SUBLANES):
        return op(x, axis=0, keepdims=True)
    slab = chains * V7X_SUBLANES
    pair = jnp.add if op is jnp.sum else jnp.maximum
    acc = x[:slab]
    for i in range(1, r // slab):
        acc = pair(acc, x[i * slab:(i + 1) * slab])
    return op(acc, axis=0, keepdims=True)


def _row_tiling(batch, length, target):
    if length >= target:
        assert length % target == 0
        return 1, target
    nb = max(1, min(batch, target // length))
    while batch % nb:
        nb -= 1
    return nb, length


def _const_spec(shape):
    zeros = (0,) * len(shape)
    return pl.BlockSpec(shape, lambda *_: zeros, pipeline_mode=pl.Buffered(1))


def _params(*sem):
    return pltpu.CompilerParams(dimension_semantics=sem, vmem_limit_bytes=V7X_VMEM_LIMIT_BYTES)


def _adaln_kernel(c_ref, w_ref, b_ref, o_ref):
    a = _silu(c_ref[...]).astype(BF16)
    o_ref[...] = _dot(a, w_ref[...]) + b_ref[...]


def _adaln(c, w_ada, b_ada):
    bt, d = c.shape
    n = w_ada.shape[1]
    tn = d
    return pl.pallas_call(
        _adaln_kernel,
        out_shape=jax.ShapeDtypeStruct((bt, n), F32),
        grid=(n // tn,),
        in_specs=[
            pl.BlockSpec((bt, d), lambda j: (0, 0)),
            pl.BlockSpec((d, tn), lambda j: (0, j)),
            pl.BlockSpec((1, tn), lambda j: (0, j)),
        ],
        out_specs=pl.BlockSpec((bt, tn), lambda j: (0, j)),
        compiler_params=_params("arbitrary"),
        name="adaln",
    )(c, w_ada, b_ada)


def _ffn_kernel(*refs, nb, rows, mod_base, with_mix, with_final_norm):
    it = iter(refs)
    x_ref, mod_ref = next(it), next(it)
    if with_mix:
        gla_ref, att_ref, wout_ref = next(it), next(it), next(it)
    g_ref, wg_ref, wu_ref, wd_ref = next(it), next(it), next(it), next(it)
    if with_final_norm:
        gout_ref = next(it)
    o_ref, h_scr, x_scr = next(it), next(it), next(it)
    tm = nb * rows

    x = x_ref[...].reshape(tm, D_MODEL)
    if with_mix:
        gla = gla_ref[...].reshape(tm, GLA_V)
        att = att_ref[...].reshape(tm, ATT_Q)
        mix = _dot(gla, wout_ref[:GLA_V, :]) + _dot(att, wout_ref[GLA_V:, :])
    g = g_ref[...]
    for j in range(nb):
        sl = slice(j * rows, (j + 1) * rows)
        m = mod_ref[j]
        xj = x[sl]
        if with_mix:
            xj = xj + m[5:6] * mix[sl]
        x_scr[sl, :] = xj
        h = (_rms(xj) * g) * (1.0 + m[mod_base + 1:mod_base + 2]) + m[mod_base:mod_base + 1]
        h_scr[sl, :] = h.astype(BF16)

    h = h_scr[...]
    acc = jnp.zeros((tm, D_MODEL), F32)
    for c in range(D_FF // _FF_CHUNK):
        cs = slice(c * _FF_CHUNK, (c + 1) * _FF_CHUNK)
        gate = _dot(h, wg_ref[:, cs])
        up = _dot(h, wu_ref[:, cs])
        a = (_silu(gate) * up).astype(BF16)
        acc = acc + _dot(a, wd_ref[cs, :])

    for j in range(nb):
        sl = slice(j * rows, (j + 1) * rows)
        m = mod_ref[j]
        y = x_scr[sl, :] + 0.5 * m[mod_base + 2:mod_base + 3] * acc[sl]
        if with_final_norm:
            y = _rms(y) * gout_ref[...]
        o_ref[j] = y


def _ffn(x, mod, g, wg, wu, wd, *, mod_base, mix=None, g_out=None, tm=512):
    b, l, d = x.shape
    nb, rows = _row_tiling(b, l, tm)
    grid = (b // nb, l // rows)
    row_spec = lambda w: pl.BlockSpec((nb, rows, w), lambda i, r: (i, r, 0))
    in_specs = [row_spec(d), pl.BlockSpec((nb, 9, d), lambda i, r: (i, 0, 0))]
    args = [x, mod]
    if mix is not None:
        gla, att, wout = mix
        in_specs += [row_spec(GLA_V), row_spec(ATT_Q), _const_spec(wout.shape)]
        args += [gla, att, wout]
    in_specs += [_const_spec(g.shape), _const_spec(wg.shape), _const_spec(wu.shape), _const_spec(wd.shape)]
    args += [g, wg, wu, wd]
    if g_out is not None:
        in_specs.append(_const_spec(g_out.shape))
        args.append(g_out)
    kern = functools.partial(
        _ffn_kernel, nb=nb, rows=rows, mod_base=mod_base,
        with_mix=mix is not None, with_final_norm=g_out is not None)
    return pl.pallas_call(
        kern,
        out_shape=jax.ShapeDtypeStruct((b, l, d), F32),
        grid=grid,
        in_specs=in_specs,
        out_specs=row_spec(d),
        scratch_shapes=[pltpu.VMEM((nb * rows, d), BF16), pltpu.VMEM((nb * rows, d), F32)],
        compiler_params=_params("parallel", "parallel"),
        name="ffn_mix" if mix is not None else "ffn",
    )(*args)


def _proj_kernel(x_ref, mod_ref, g_ref, w_ref, wa2_ref, ba_ref, gq_ref, gk_ref, bd_ref,
                 qk_ref, v_ref, sgr_ref, la_ref, qatt_ref, knew_ref, vnew_ref, vtnew_ref,
                 iknew_ref, iq_ref, iwt_ref, h_scr, *, nb, rows):
    tm = nb * rows
    x = x_ref[...].reshape(tm, D_MODEL)
    g = g_ref[...]
    for j in range(nb):
        sl = slice(j * rows, (j + 1) * rows)
        m = mod_ref[j]
        h = (_rms(x[sl]) * g) * (1.0 + m[4:5]) + m[3:4]
        h_scr[sl, :] = h.astype(BF16)
    h = h_scr[...]

    def seg(start, width):
        return _dot(h, w_ref[:, start:start + width])

    def put(ref, val, width):
        ref[...] = val.reshape(nb, rows, width).astype(ref.dtype)

    aq = seg(_SEG_AQ, ATT_Q)
    akv = seg(_SEG_AKV, 2 * ATT_KV)
    misc = seg(_SEG_MISC, V7X_LANES)

    lane = lax.broadcasted_iota(I32, (1, 2 * GLA_QK), 1)
    qscale = jnp.where(lane < GLA_QK, GLA_DK ** -0.5, 1.0).astype(F32)
    put(qk_ref, seg(_SEG_GQK, 2 * GLA_QK) * qscale, 2 * GLA_QK)
    put(v_ref, seg(_SEG_GV, GLA_V), GLA_V)
    put(sgr_ref, _silu(seg(_SEG_GR, GLA_V)), GLA_V)
    put(iq_ref, seg(_SEG_IQ, IDX_Q), IDX_Q)

    ak, av = akv[:, :ATT_KV], akv[:, ATT_KV:]
    aq2 = (aq * aq).astype(BF16)
    msq = jnp.concatenate(
        [_dot(aq2[:, i * V7X_LANES:(i + 1) * V7X_LANES], bd_ref[:V7X_LANES, :V7X_LANES])
         for i in range(ATT_Q // V7X_LANES)], axis=1) * (1.0 / ATT_HD)
    msk = _dot((ak * ak).astype(BF16), bd_ref[:ATT_KV, :ATT_KV]) * (1.0 / ATT_HD)
    za = _dot(misc.astype(BF16), wa2_ref[...]) + ba_ref[...]
    put(qatt_ref, aq * lax.rsqrt(msq + EPS) * gq_ref[...] * (ATT_HD ** -0.5 * _LOG2E), ATT_Q)
    put(knew_ref, ak * lax.rsqrt(msk + EPS) * gk_ref[...], ATT_KV)
    put(vnew_ref, av, ATT_KV)
    avt = av.T
    for j in range(nb):
        vtnew_ref[j] = avt[:, j * rows:(j + 1) * rows].astype(BF16)

    put(iknew_ref, misc[:, :IDX_DIM], IDX_DIM)
    log_sig = jnp.minimum(za, 0.0) - jnp.log(1.0 + jnp.exp(-jnp.abs(za)))
    put(la_ref, log_sig * (1.0 / GLA_TAU), GLA_QK)
    misct = misc.T
    iwt = misct[_MISC_IW:_MISC_IW + _IWT_ROWS, :] * ((IDX_HEADS * IDX_DIM) ** -0.5)
    for j in range(nb):
        iwt_ref[j] = iwt[:, j * rows:(j + 1) * rows]


def _proj(x, mod, g_mix, w_packed, wa2, b_a, gq_t, gk_t, bd, *, tm=512):
    b, l, d = x.shape
    nb, rows = _row_tiling(b, l, tm)
    grid = (b // nb, l // rows)
    row_spec = lambda w: pl.BlockSpec((nb, rows, w), lambda i, r: (i, r, 0))
    col_spec = lambda h: pl.BlockSpec((nb, h, rows), lambda i, r: (i, 0, r))
    sds = lambda w, dt: jax.ShapeDtypeStruct((b, l, w), dt)
    out_shape = [
        sds(2 * GLA_QK, F32), sds(GLA_V, BF16), sds(GLA_V, BF16), sds(GLA_QK, F32), sds(ATT_Q, BF16),
        sds(ATT_KV, F32), sds(ATT_KV, F32), jax.ShapeDtypeStruct((b, ATT_KV, l), BF16),
        sds(IDX_DIM, F32), sds(IDX_Q, F32), jax.ShapeDtypeStruct((b, _IWT_ROWS, l), F32),
    ]
    out_specs = [
        row_spec(2 * GLA_QK), row_spec(GLA_V), row_spec(GLA_V), row_spec(GLA_QK), row_spec(ATT_Q),
        row_spec(ATT_KV), row_spec(ATT_KV), col_spec(ATT_KV),
        row_spec(IDX_DIM), row_spec(IDX_Q), col_spec(_IWT_ROWS),
    ]
    consts = [g_mix, w_packed, wa2, b_a, gq_t, gk_t, bd]
    return pl.pallas_call(
        functools.partial(_proj_kernel, nb=nb, rows=rows),
        out_shape=out_shape,
        grid=grid,
        in_specs=[row_spec(d), pl.BlockSpec((nb, 9, d), lambda i, r: (i, 0, 0))]
        + [_const_spec(c.shape) for c in consts],
        out_specs=out_specs,
        scratch_shapes=[pltpu.VMEM((nb * rows, d), BF16)],
        compiler_params=_params("parallel", "parallel"),
        name="proj",
    )(x, mod, *consts)


def _gla_kernel(*refs, has_s0, nbb):
    it = iter(refs)
    qk_ref, v_ref, la_ref, sgr_ref = next(it), next(it), next(it), next(it)
    s0_ref = next(it) if has_s0 else None
    g_ref, bd_ref, o_ref, st_ref, s_scr = next(it), next(it), next(it), next(it), next(it)
    t, cb = _GLA_T, GLA_BLOCK
    nsb = t // cb
    c = pl.program_id(1)

    @pl.when(c == 0)
    def _():
        s_scr[...] = jnp.zeros((nbb, GLA_QK, GLA_V), F32)
        if has_s0:
            for j in range(nbb):
                for h in range(GLA_HEADS):
                    s_scr[j, h * GLA_DK:(h + 1) * GLA_DK, h * GLA_DV:(h + 1) * GLA_DV] = s0_ref[j, h]

    ri = lax.broadcasted_iota(I32, (t, t), 0)
    ci = lax.broadcasted_iota(I32, (t, t), 1)
    tril = jnp.where(ri >= ci, 1.0, 0.0).astype(BF16)
    lane = lax.broadcasted_iota(I32, (1, GLA_QK), 1)
    head_mask = [jnp.where((lane >= h * GLA_DK) & (lane < (h + 1) * GLA_DK), 1.0, 0.0).astype(F32)
                 for h in range(GLA_HEADS)]
    g = g_ref[...]


    cum = []
    for j in range(nbb):
        la = la_ref[j]
        p0 = la.astype(BF16)
        r0 = la - p0.astype(F32)
        p1 = r0.astype(BF16)
        p2 = (r0 - p1.astype(F32)).astype(BF16)
        bb = _dot(tril, jnp.concatenate([p0, p1, p2], axis=1))
        cum.append(bb[:, :GLA_QK] + bb[:, GLA_QK:2 * GLA_QK] + bb[:, 2 * GLA_QK:])

    o_inter, scores = [], []
    for j in range(nbb):
        b = cum[j]
        qk = qk_ref[j]
        q, k = qk[:, :GLA_QK], qk[:, GLA_QK:]
        v = v_ref[j]
        b0 = [jnp.zeros((1, GLA_QK), F32)] + [b[cb * i - 1:cb * i, :] for i in range(1, nsb)]
        btot = b[t - 1:t, :]
        bstart = jnp.concatenate([jnp.broadcast_to(b0[i], (cb, GLA_QK)) for i in range(nsb)], axis=0)
        q_rel = q * jnp.exp(b - bstart)
        q_int = (q * jnp.exp(b)).astype(BF16)
        k_end = k * jnp.exp(btot - b)

        s_old = s_scr[j]
        o_inter.append(_dot(q_int, s_old.astype(BF16)))
        ds = _dot(k_end.T.astype(BF16), v)
        dcol = jnp.exp(jnp.broadcast_to(btot, (V7X_SUBLANES, GLA_QK)).T[:, 0:1])
        s_scr[j] = s_old * dcol + ds * bd_ref[...]

        sc = []
        for i in range(nsb):
            n = cb * (i + 1)
            qs = q_rel[cb * i:cb * (i + 1)]
            q_stack = jnp.concatenate([qs * head_mask[h] for h in range(GLA_HEADS)], axis=0).astype(BF16)
            km = (k[:n] * jnp.exp(b0[i] - b[:n])).astype(BF16)
            sc.append(_dot_nt(q_stack, km))
        scores.append(sc)

    for j in range(nbb):
        v = v_ref[j]
        intra = []
        for i in range(nsb):
            n = cb * (i + 1)
            rr = lax.broadcasted_iota(I32, (GLA_HEADS * cb, n), 0)
            cc = lax.broadcasted_iota(I32, (GLA_HEADS * cb, n), 1)
            a = jnp.where(cc - cb * i <= (rr & (cb - 1)), scores[j][i], 0.0).astype(BF16)
            oi = _dot(a, v[:n])
            intra.append(jnp.concatenate(
                [oi[h * cb:(h + 1) * cb, h * GLA_DV:(h + 1) * GLA_DV] for h in range(GLA_HEADS)], axis=1))
        o = o_inter[j] + jnp.concatenate(intra, axis=0)
        on = jnp.concatenate([_rms(o[:, h * GLA_DV:(h + 1) * GLA_DV]) * g for h in range(GLA_HEADS)], axis=1)
        o_ref[j] = (on * sgr_ref[j].astype(F32)).astype(o_ref.dtype)

    @pl.when(c == pl.num_programs(1) - 1)
    def _():
        for j in range(nbb):
            for h in range(GLA_HEADS):
                st_ref[j, h] = s_scr[j, h * GLA_DK:(h + 1) * GLA_DK, h * GLA_DV:(h + 1) * GLA_DV]


def _gla(qk, v, la, sgr, s0, g_gla, bd_state, *, nbb=8):
    b, l, _ = qk.shape
    assert l % _GLA_T == 0
    while b % nbb:
        nbb -= 1
    tok = lambda w: pl.BlockSpec((nbb, _GLA_T, w), lambda i, c: (i, c, 0))
    st_spec = pl.BlockSpec((nbb, GLA_HEADS, GLA_DK, GLA_DV), lambda i, c: (i, 0, 0, 0))
    in_specs = [tok(2 * GLA_QK), tok(GLA_V), tok(GLA_QK), tok(GLA_V)]
    args = [qk, v, la, sgr]
    if s0 is not None:
        in_specs.append(st_spec)
        args.append(s0)
    in_specs += [_const_spec(g_gla.shape), _const_spec(bd_state.shape)]
    args += [g_gla, bd_state]
    return pl.pallas_call(
        functools.partial(_gla_kernel, has_s0=s0 is not None, nbb=nbb),
        out_shape=[jax.ShapeDtypeStruct((b, l, GLA_V), BF16),
                   jax.ShapeDtypeStruct((b, GLA_HEADS, GLA_DK, GLA_DV), F32)],
        grid=(b // nbb, l // _GLA_T),
        in_specs=in_specs,
        out_specs=[tok(GLA_V), st_spec],
        scratch_shapes=[pltpu.VMEM((nbb, GLA_QK, GLA_V), F32)],
        compiler_params=_params("parallel", "arbitrary"),
        name="gla",
    )(*args)


def _dsa_kernel(*refs, p_len, e_new, tq, topk):
    has_past = p_len > 0
    it = iter(refs)
    if has_past:
        kp_ref, vp_ref, ikp_ref = next(it), next(it), next(it)
    kn_ref, vtn_ref, ikn_ref, q_ref, iq_ref, iwt_ref = (next(it) for _ in range(6))
    o_ref, score_scr, bias_scr = (next(it) for _ in range(3))
    e = p_len + e_new
    w0 = e - tq

    k_all, ik_all, vt = kn_ref[0], ikn_ref[0], vtn_ref[0]
    if has_past:
        k_all = jnp.concatenate([kp_ref[0], k_all], axis=0)
        ik_all = jnp.concatenate([ikp_ref[0], ik_all], axis=0)
        vt = jnp.concatenate([vp_ref[0].T.astype(BF16), vt], axis=1)

    def on_window(full, fn):
        return jnp.concatenate([full[:w0], fn(full[w0:])], axis=0) if w0 else fn(full)

    wr = lax.broadcasted_iota(I32, (tq, tq), 0)
    wc = lax.broadcasted_iota(I32, (tq, tq), 1)
    limit = (((w0 + wc) >> _CHUNK_SHIFT) + 1) << _CHUNK_SHIFT
    adm_win = (w0 + wr) < limit

    ik_hi = ik_all.astype(BF16)
    ik_lo = (ik_all - ik_hi.astype(F32)).astype(BF16)
    ik3 = jnp.concatenate([ik_hi, ik_lo, ik_hi], axis=1)
    iq = iq_ref[0]
    iwt = iwt_ref[0]
    wide = tq % V7X_LANES == 0

    def iq_pieces(h):
        iqh = iq[:, h * IDX_DIM:(h + 1) * IDX_DIM]
        hi = iqh.astype(BF16)
        lo = (iqh - hi.astype(F32)).astype(BF16)
        return jnp.concatenate([hi, hi, lo], axis=1)

    if wide:
        logits = _dot_nt(ik3, jnp.concatenate([iq_pieces(h) for h in range(IDX_HEADS)], axis=0))
        logit = [logits[:, h * tq:(h + 1) * tq] for h in range(IDX_HEADS)]
    else:
        logit = [_dot_nt(ik3, iq_pieces(h)) for h in range(IDX_HEADS)]
    score = jnp.zeros((e, tq), F32)
    for h in range(IDX_HEADS):
        score = score + iwt[h:h + 1, :] * jnp.maximum(logit[h], 0.0)
    score_scr[...] = on_window(score, lambda s: jnp.where(adm_win, s, -jnp.inf))

    def key_to_f32(c):
        return lax.bitcast_convert_type(c ^ ((c >> 31) & 0x7FFFFFFF), F32)

    def count_ge(cand):
        cf = key_to_f32(cand)
        slab = 8 * V7X_SUBLANES
        if e % slab:
            return _col_reduce(jnp.where(score_scr[...] >= cf, 1.0, 0.0), jnp.sum)
        acc = jnp.zeros((slab, tq), F32)
        for i in range(e // slab):
            acc = jnp.where(score_scr[i * slab:(i + 1) * slab, :] >= cf, acc + 1.0, acc)
        return jnp.sum(acc, axis=0, keepdims=True)

    c0 = count_ge(jnp.zeros((1, tq), I32))
    thr = jnp.where(c0 >= topk, 0, _INT_MIN).astype(I32)
    cnt = jnp.where(c0 >= topk, c0, float(e))

    def thr_step(i, carry):
        thr, cnt = carry
        cand = thr | lax.shift_left(jnp.int32(1), 30 - i)
        c = count_ge(cand)
        ok = c >= topk
        return jnp.where(ok, cand, thr), jnp.where(ok, c, cnt)

    thr, cnt = lax.fori_loop(0, 31, thr_step, (thr, cnt))
    thr_f = jnp.where(thr == _INT_MIN, -jnp.inf, key_to_f32(thr))

    sc = score_scr[...]
    blk = next(b for b in (256, 128, 64, 32, 16, 8) if e % b == 0)
    ri = lax.broadcasted_iota(I32, (blk, blk), 0)
    ci = lax.broadcasted_iota(I32, (blk, blk), 1)
    tri = jnp.where(ri >= ci, 1.0, 0.0).astype(BF16)
    n_tied = jnp.zeros((1, tq), F32)
    ranks = []
    for j in range(e // blk):
        tied = jnp.where(sc[j * blk:(j + 1) * blk] == thr_f, 1.0, 0.0).astype(BF16)
        ranks.append(_dot(tri, tied) + n_tied)
        n_tied = ranks[-1][blk - 1:blk, :]
    need = topk - (cnt - n_tied)
    sel_bias = jnp.concatenate([
        jnp.where(sc[j * blk:(j + 1) * blk] > thr_f, 0.0,
                  jnp.where(sc[j * blk:(j + 1) * blk] == thr_f, jnp.where(ranks[j] <= need, 0.0, _NEG), _NEG))
        for j in range(e // blk)], axis=0)
    bias_scr[...] = on_window(sel_bias, lambda s: jnp.where(adm_win, s, _NEG))

    assert e <= _POS_SPLIT * _POS_SPLIT
    pr = lax.broadcasted_iota(I32, (e, ATT_HD), 0)
    pc = lax.broadcasted_iota(I32, (e, ATT_HD), 1)
    pos_lo = pr & (_POS_SPLIT - 1)
    pos_cols = jnp.where(pc < 2 * _SLOPE_PIECES, jnp.where((pc & 1) == 0, pr - pos_lo, pos_lo), 0)
    pos_cols = pos_cols.astype(F32).astype(BF16)
    k_aug = [jnp.concatenate([k_all[:, g * ATT_HD:(g + 1) * ATT_HD].astype(BF16), pos_cols], axis=1)
             for g in range(ATT_KV_HEADS)]
    ahead = -2.0 * jnp.maximum(wr - wc, 0).astype(F32)
    qc = lax.broadcasted_iota(I32, (tq, ATT_HD), 1)
    q = q_ref[0]

    def slope_of(h):
        return jnp.float32(_LOG2E * 2.0 ** (-8.0 * (h + 1) / ATT_HEADS))

    def q_aug(h):
        rest = jnp.full((tq, ATT_HD), slope_of(h), F32)
        cols = jnp.zeros((tq, ATT_HD), F32)
        for i in range(_SLOPE_PIECES):
            piece = rest.astype(BF16).astype(F32)
            rest = rest - piece
            cols = jnp.where((qc >> 1) == i, piece, cols)
        return jnp.concatenate([q[:, h * ATT_HD:(h + 1) * ATT_HD], cols.astype(BF16)], axis=1)

    def probs_of(s, h):
        s = on_window(s, lambda sw: sw + slope_of(h) * ahead) + bias_scr[...]
        return jnp.exp2(s - _col_reduce(s, jnp.max)).astype(BF16)

    ones_rows = jnp.ones((_BF16_SUBLANES, e), BF16)
    vt_aug = [jnp.concatenate([vt[g * ATT_HD:(g + 1) * ATT_HD, :], ones_rows], axis=0) for g in range(ATT_KV_HEADS)]

    def normalised(ot):
        return ot[:ATT_HD] / ot[ATT_HD:ATT_HD + 1]

    outs = []
    if wide:
        group_logits = [
            _dot_nt(k_aug[g], jnp.concatenate([q_aug(g * ATT_REP + r) for r in range(ATT_REP)], axis=0))
            for g in range(ATT_KV_HEADS)]
        for g in range(ATT_KV_HEADS):
            probs = [probs_of(group_logits[g][:, r * tq:(r + 1) * tq], g * ATT_REP + r) for r in range(ATT_REP)]
            ot = _dot(vt_aug[g], jnp.concatenate(probs, axis=1))
            outs += [normalised(ot[:, r * tq:(r + 1) * tq]) for r in range(ATT_REP)]
    else:
        ahead_heads = max(1, min(ATT_HEADS - 1, _DSA_LOGIT_ROWS_IN_FLIGHT // e))
        qk = lambda h: _dot_nt(k_aug[h // ATT_REP], q_aug(h))
        pending = [qk(h) for h in range(ahead_heads)]
        for h in range(ATT_HEADS):
            s = pending.pop(0)
            if h + ahead_heads < ATT_HEADS:
                pending.append(qk(h + ahead_heads))
            outs.append(normalised(_dot(vt_aug[h // ATT_REP], probs_of(s, h))))
    ot = jnp.concatenate(outs, axis=0)
    if tq % V7X_LANES:
        ot = jnp.concatenate([ot, jnp.zeros((ATT_Q, V7X_LANES - tq % V7X_LANES), F32)], axis=1)
    o_ref[0] = ot.T[:tq].astype(o_ref.dtype)


def _dsa_call(q_att, iq, iwt, k_new, vt_new, ik_new, past, *, tq, tile, topk):
    b = q_att.shape[0]
    p_len = 0 if past is None else past[0].shape[1]
    e_new = (tile + 1) * tq
    e = p_len + e_new
    in_specs, args = [], []
    if past is not None:
        in_specs += [pl.BlockSpec((1, p_len, w), lambda i: (i, 0, 0)) for w in (ATT_KV, ATT_KV, IDX_DIM)]
        args += list(past)
    in_specs += [
        pl.BlockSpec((1, e_new, ATT_KV), lambda i: (i, 0, 0)),
        pl.BlockSpec((1, ATT_KV, e_new), lambda i: (i, 0, 0)),
        pl.BlockSpec((1, e_new, IDX_DIM), lambda i: (i, 0, 0)),
        pl.BlockSpec((1, tq, ATT_Q), lambda i: (i, tile, 0)),
        pl.BlockSpec((1, tq, IDX_Q), lambda i: (i, tile, 0)),
        pl.BlockSpec((1, _IWT_ROWS, tq), lambda i: (i, 0, tile)),
    ]
    args += [k_new, vt_new, ik_new, q_att, iq, iwt]
    kern = functools.partial(_dsa_kernel, p_len=p_len, e_new=e_new, tq=tq, topk=topk)
    return pl.pallas_call(
        kern,
        out_shape=jax.ShapeDtypeStruct((b, tq, ATT_Q), BF16),
        grid=(b,),
        in_specs=in_specs,
        out_specs=pl.BlockSpec((1, tq, ATT_Q), lambda i: (i, 0, 0)),
        scratch_shapes=[
            pltpu.VMEM((e, tq), F32),
            pltpu.VMEM((e, tq), F32),
        ],
        compiler_params=_params("parallel"),
        name=f"dsa_e{e}",
    )(*args)


def _dsa(q_att, iq, iwt, k_new, vt_new, ik_new, past):
    b, l, _ = q_att.shape
    p_len = 0 if past is None else past[0].shape[1]
    topk = min(TOPK_MAX, (p_len + l) // 4)
    if l <= CHUNK:
        return _dsa_call(q_att, iq, iwt, k_new, vt_new, ik_new, past, tq=l, tile=0, topk=topk)
    assert l % _DSA_TQ == 0
    outs = [_dsa_call(q_att, iq, iwt, k_new, vt_new, ik_new, past, tq=_DSA_TQ, tile=t, topk=topk)
            for t in range(l // _DSA_TQ)]
    return jnp.concatenate(outs, axis=1)


def _block_diag_ones(n, blk):
    i = jnp.arange(n) // blk
    return (i[:, None] == i[None, :])


def _prep_weights(w_ada, b_ada, g_ffn1, w1_gate, w1_up, w1_down, g_mix, w_in, w_a2, b_a, g_gla,
                  g_q, g_k, w_out, g_ffn2, w2_gate, w2_up, w2_down, g_out):
    sizes = (GLA_QK, GLA_QK, GLA_V, GLA_LOWRANK, GLA_V, ATT_Q, ATT_KV, ATT_KV, IDX_Q, IDX_DIM, IDX_HEADS)
    pts, acc = [], 0
    for s in sizes[:-1]:
        acc += s
        pts.append(acc)
    gq, gk, gv, ga, gr, aq, ak, av, iq, ik, iw = jnp.split(w_in, pts, axis=1)
    pad = jnp.zeros((D_MODEL, _PACKED_WIDTH - sum(sizes)), w_in.dtype)
    w_packed = jnp.concatenate([gq, gk, gv, gr, aq, ak, av, iq, ik, ga, iw, pad], axis=1).astype(BF16)
    wa2 = jnp.zeros((V7X_LANES, GLA_QK), F32).at[_MISC_GA:_MISC_GA + GLA_LOWRANK].set(w_a2).astype(BF16)
    row = lambda a: a.reshape(1, -1).astype(F32)
    bd_state = jnp.repeat(jnp.repeat(jnp.eye(GLA_HEADS, dtype=F32), GLA_DK, axis=0), GLA_DV, axis=1)
    return dict(
        w_ada=w_ada.astype(BF16), b_ada=row(b_ada),
        g_ffn1=row(g_ffn1), w1=(w1_gate.astype(BF16), w1_up.astype(BF16), w1_down.astype(BF16)),
        g_mix=row(g_mix), w_packed=w_packed, wa2=wa2, b_a=row(b_a),
        g_gla=row(g_gla), gq_t=row(jnp.tile(g_q, ATT_HEADS)), gk_t=row(jnp.tile(g_k, ATT_KV_HEADS)),
        bd_heads=_block_diag_ones(ATT_Q, ATT_HD).astype(BF16), bd_state=bd_state,
        w_out=w_out.astype(BF16),
        g_ffn2=row(g_ffn2), w2=(w2_gate.astype(BF16), w2_up.astype(BF16), w2_down.astype(BF16)),
        g_out=row(g_out),
    )


def _layer(x, mod, past, s0, w):
    b, l, _ = x.shape
    x1 = _ffn(x, mod, w["g_ffn1"], *w["w1"], mod_base=0)
    (qk, gv, sgr, la, q_att, k_new, v_new, vt_new, ik_new, iq, iwt) = _proj(
        x1, mod, w["g_mix"], w["w_packed"], w["wa2"], w["b_a"], w["gq_t"], w["gk_t"], w["bd_heads"])
    gla_out, s_t = _gla(qk, gv, la, sgr, s0, w["g_gla"], w["bd_state"])
    att = _dsa(q_att, iq, iwt, k_new, vt_new, ik_new, past)
    y = _ffn(x1, mod, w["g_ffn2"], *w["w2"], mod_base=6, mix=(gla_out, att, w["w_out"]), g_out=w["g_out"])
    kv_shape = (b, l, ATT_KV_HEADS, ATT_HD)
    return y, k_new.reshape(kv_shape), v_new.reshape(kv_shape), ik_new, s_t


def kernel(x_prompt, x_sample, c_prompt, c_sample, cache_k, cache_v, cache_idx_k, state_gla, w_ada, b_ada, g_ffn1, w1_gate, w1_up, w1_down, g_mix, w_in, w_a2, b_a, g_gla, g_q, g_k, w_out, g_ffn2, w2_gate, w2_up, w2_down, g_out):
    depth = w_ada.shape[0]
    bp = x_prompt.shape[0]
    yp, ys = x_prompt, x_sample
    outs_p, outs_s = [], []
    for layer in range(depth):
        w = _prep_weights(*(t[layer] for t in (
            w_ada, b_ada, g_ffn1, w1_gate, w1_up, w1_down, g_mix, w_in, w_a2, b_a, g_gla,
            g_q, g_k, w_out, g_ffn2, w2_gate, w2_up, w2_down, g_out)))
        mod = _adaln(jnp.concatenate([c_prompt, c_sample], axis=0), w["w_ada"], w["b_ada"])
        mod = mod.reshape(mod.shape[0], 9, D_MODEL)
        ds, pp = cache_k.shape[1], cache_k.shape[2]
        past = (cache_k[layer].reshape(ds, pp, ATT_KV), cache_v[layer].reshape(ds, pp, ATT_KV), cache_idx_k[layer])
        yp, *rest_p = _layer(yp, mod[:bp], None, None, w)
        ys, *rest_s = _layer(ys, mod[bp:], past, state_gla[layer], w)
        outs_p.append(rest_p)
        outs_s.append(rest_s)
    stack = lambda outs, i: jnp.stack([o[i] for o in outs])
    return (yp, ys,
            stack(outs_p, 0), stack(outs_p, 1), stack(outs_p, 2), stack(outs_p, 3),
            stack(outs_s, 0), stack(outs_s, 1), stack(outs_s, 2), stack(outs_s, 3))
```

```python
import functools

import jax
import jax.numpy as jnp
from jax import lax
from jax.experimental import pallas as pl
from jax.experimental.pallas import tpu as pltpu

F32 = jnp.float32
BF16 = jnp.bfloat16
I32 = jnp.int32

D_MODEL = 1024
D_FF = 2816
CHUNK = 64
GLA_HEADS = 4
GLA_DK = 64
GLA_DV = 128
GLA_LOWRANK = 16
GLA_TAU = 16.0
GLA_BLOCK = 16
ATT_HEADS = 8
ATT_KV_HEADS = 2
ATT_HD = 64
IDX_HEADS = 4
IDX_DIM = 64
TOPK_MAX = 256
EPS = 1e-6

GLA_QK = GLA_HEADS * GLA_DK
GLA_V = GLA_HEADS * GLA_DV
ATT_Q = ATT_HEADS * ATT_HD
ATT_KV = ATT_KV_HEADS * ATT_HD
IDX_Q = IDX_HEADS * IDX_DIM
MIX_WIDTH = GLA_V + ATT_Q
ATT_REP = ATT_HEADS // ATT_KV_HEADS

V7X_LANES = 128
V7X_SUBLANES = 8
V7X_VMEM_LIMIT_BYTES = 56 * 1024 * 1024

_SEG_GQK = 0
_SEG_GV = 512
_SEG_GR = 1024
_SEG_AQ = 1536
_SEG_AKV = 2048
_SEG_IQ = 2304
_SEG_MISC = 2560
_PACKED_WIDTH = 2688
_MISC_GA = IDX_DIM
_MISC_IW = IDX_DIM + GLA_LOWRANK
_IWT_ROWS = V7X_SUBLANES

_FF_CHUNK = 256
_GLA_T = 128
_DSA_TQ = 256
_DSA_LOGIT_ROWS_IN_FLIGHT = 4096
_NEG = -1e30
_INT_MIN = -(2 ** 31)
_CHUNK_SHIFT = CHUNK.bit_length() - 1
_POS_SPLIT = 256
_SLOPE_PIECES = 3
_LOG2E = 1.4426950408889634
_BF16_SUBLANES = 2 * V7X_SUBLANES
assert 1 << _CHUNK_SHIFT == CHUNK

_NT = (((1,), (1,)), ((), ()))


def _dot(a, b):
    return jnp.dot(a, b, preferred_element_type=F32)


def _dot_nt(a, b):
    return lax.dot_general(a, b, _NT, preferred_element_type=F32)


def _rms(x):
    return x * lax.rsqrt(jnp.mean(x * x, axis=-1, keepdims=True) + EPS)


def _silu(x):
    return x * jax.nn.sigmoid(x)


def _col_reduce(x, op, chains=8):
    r, n = x.shape
    while chains > 1 and r % (chains * V7X_SUBLANES):
        chains //= 2
    if r % (chains * V7X_SUBLANES):
        return op(x, axis=0, keepdims=True)
    slab = chains * V7X_SUBLANES
    pair = jnp.add if op is jnp.sum else jnp.maximum
    acc = x[:slab]
    for i in range(1, r // slab):
        acc = pair(acc, x[i * slab:(i + 1) * slab])
    return op(acc, axis=0, keepdims=True)


def _row_tiling(batch, length, target):
    if length >= target:
        assert length % target == 0
        return 1, target
    nb = max(1, min(batch, target // length))
    while batch % nb:
        nb -= 1
    return nb, length


def _const_spec(shape):
    zeros = (0,) * len(shape)
    return pl.BlockSpec(shape, lambda *_: zeros, pipeline_mode=pl.Buffered(1))


def _params(*sem):
    return pltpu.CompilerParams(dimension_semantics=sem, vmem_limit_bytes=V7X_VMEM_LIMIT_BYTES)


def _adaln_kernel(c_ref, w_ref, b_ref, o_ref):
    a = _silu(c_ref[...]).astype(BF16)
    o_ref[...] = _dot(a, w_ref[...]) + b_ref[...]


def _adaln(c, w_ada, b_ada):
    bt, d = c.shape
    n = w_ada.shape[1]
    tn = d
    return pl.pallas_call(
        _adaln_kernel,
        out_shape=jax.ShapeDtypeStruct((bt, n), F32),
        grid=(n // tn,),
        in_specs=[
            pl.BlockSpec((bt, d), lambda j: (0, 0)),
            pl.BlockSpec((d, tn), lambda j: (0, j)),
            pl.BlockSpec((1, tn), lambda j: (0, j)),
        ],
        out_specs=pl.BlockSpec((bt, tn), lambda j: (0, j)),
        compiler_params=_params("arbitrary"),
        name="adaln",
    )(c, w_ada, b_ada)


def _ffn_kernel(*refs, nb, rows, mod_base, with_mix, with_final_norm):
    it = iter(refs)
    x_ref, mod_ref = next(it), next(it)
    if with_mix:
        gla_ref, att_ref, wout_ref = next(it), next(it), next(it)
    g_ref, wg_ref, wu_ref, wd_ref = next(it), next(it), next(it), next(it)
    if with_final_norm:
        gout_ref = next(it)
    o_ref, h_scr, x_scr = next(it), next(it), next(it)
    tm = nb * rows

    x = x_ref[...].reshape(tm, D_MODEL)
    if with_mix:
        gla = gla_ref[...].reshape(tm, GLA_V)
        att = att_ref[...].reshape(tm, ATT_Q)
        mix = _dot(gla, wout_ref[:GLA_V, :]) + _dot(att, wout_ref[GLA_V:, :])
    g = g_ref[...]
    for j in range(nb):
        sl = slice(j * rows, (j + 1) * rows)
        m = mod_ref[j]
        xj = x[sl]
        if with_mix:
            xj = xj + m[5:6] * mix[sl]
        x_scr[sl, :] = xj
        h = (_rms(xj) * g) * (1.0 + m[mod_base + 1:mod_base + 2]) + m[mod_base:mod_base + 1]
        h_scr[sl, :] = h.astype(BF16)

    h = h_scr[...]
    acc = jnp.zeros((tm, D_MODEL), F32)
    for c in range(D_FF // _FF_CHUNK):
        cs = slice(c * _FF_CHUNK, (c + 1) * _FF_CHUNK)
        gate = _dot(h, wg_ref[:, cs])
        up = _dot(h, wu_ref[:, cs])
        a = (_silu(gate) * up).astype(BF16)
        acc = acc + _dot(a, wd_ref[cs, :])

    for j in range(nb):
        sl = slice(j * rows, (j + 1) * rows)
        m = mod_ref[j]
        y = x_scr[sl, :] + 0.5 * m[mod_base + 2:mod_base + 3] * acc[sl]
        if with_final_norm:
            y = _rms(y) * gout_ref[...]
        o_ref[j] = y


def _ffn(x, mod, g, wg, wu, wd, *, mod_base, mix=None, g_out=None, tm=512):
    b, l, d = x.shape
    nb, rows = _row_tiling(b, l, tm)
    grid = (b // nb, l // rows)
    row_spec = lambda w: pl.BlockSpec((nb, rows, w), lambda i, r: (i, r, 0))
    in_specs = [row_spec(d), pl.BlockSpec((nb, 9, d), lambda i, r: (i, 0, 0))]
    args = [x, mod]
    if mix is not None:
        gla, att, wout = mix
        in_specs += [row_spec(GLA_V), row_spec(ATT_Q), _const_spec(wout.shape)]
        args += [gla, att, wout]
    in_specs += [_const_spec(g.shape), _const_spec(wg.shape), _const_spec(wu.shape), _const_spec(wd.shape)]
    args += [g, wg, wu, wd]
    if g_out is not None:
        in_specs.append(_const_spec(g_out.shape))
        args.append(g_out)
    kern = functools.partial(
        _ffn_kernel, nb=nb, rows=rows, mod_base=mod_base,
        with_mix=mix is not None, with_final_norm=g_out is not None)
    return pl.pallas_call(
        kern,
        out_shape=jax.ShapeDtypeStruct((b, l, d), F32),
        grid=grid,
        in_specs=in_specs,
        out_specs=row_spec(d),
        scratch_shapes=[pltpu.VMEM((nb * rows, d), BF16), pltpu.VMEM((nb * rows, d), F32)],
        compiler_params=_params("parallel", "parallel"),
        name="ffn_mix" if mix is not None else "ffn",
    )(*args)


def _proj_kernel(x_ref, mod_ref, g_ref, w_ref, wa2_ref, ba_ref, gq_ref, gk_ref, bd_ref,
                 qk_ref, v_ref, sgr_ref, la_ref, qatt_ref, knew_ref, vnew_ref, vtnew_ref,
                 iknew_ref, iq_ref, iwt_ref, h_scr, *, nb, rows):
    tm = nb * rows
    x = x_ref[...].reshape(tm, D_MODEL)
    g = g_ref[...]
    for j in range(nb):
        sl = slice(j * rows, (j + 1) * rows)
        m = mod_ref[j]
        h = (_rms(x[sl]) * g) * (1.0 + m[4:5]) + m[3:4]
        h_scr[sl, :] = h.astype(BF16)
    h = h_scr[...]

    def seg(start, width):
        return _dot(h, w_ref[:, start:start + width])

    def put(ref, val, width):
        ref[...] = val.reshape(nb, rows, width).astype(ref.dtype)

    aq = seg(_SEG_AQ, ATT_Q)
    akv = seg(_SEG_AKV, 2 * ATT_KV)
    misc = seg(_SEG_MISC, V7X_LANES)

    lane = lax.broadcasted_iota(I32, (1, 2 * GLA_QK), 1)
    qscale = jnp.where(lane < GLA_QK, GLA_DK ** -0.5, 1.0).astype(F32)
    put(qk_ref, seg(_SEG_GQK, 2 * GLA_QK) * qscale, 2 * GLA_QK)
    put(v_ref, seg(_SEG_GV, GLA_V), GLA_V)
    put(sgr_ref, _silu(seg(_SEG_GR, GLA_V)), GLA_V)
    put(iq_ref, seg(_SEG_IQ, IDX_Q), IDX_Q)

    ak, av = akv[:, :ATT_KV], akv[:, ATT_KV:]
    aq2 = (aq * aq).astype(BF16)
    msq = jnp.concatenate(
        [_dot(aq2[:, i * V7X_LANES:(i + 1) * V7X_LANES], bd_ref[:V7X_LANES, :V7X_LANES])
         for i in range(ATT_Q // V7X_LANES)], axis=1) * (1.0 / ATT_HD)
    msk = _dot((ak * ak).astype(BF16), bd_ref[:ATT_KV, :ATT_KV]) * (1.0 / ATT_HD)
    za = _dot(misc.astype(BF16), wa2_ref[...]) + ba_ref[...]
    put(qatt_ref, aq * lax.rsqrt(msq + EPS) * gq_ref[...] * (ATT_HD ** -0.5 * _LOG2E), ATT_Q)
    put(knew_ref, ak * lax.rsqrt(msk + EPS) * gk_ref[...], ATT_KV)
    put(vnew_ref, av, ATT_KV)
    avt = av.T
    for j in range(nb):
        vtnew_ref[j] = avt[:, j * rows:(j + 1) * rows].astype(BF16)

    put(iknew_ref, misc[:, :IDX_DIM], IDX_DIM)
    log_sig = jnp.minimum(za, 0.0) - jnp.log(1.0 + jnp.exp(-jnp.abs(za)))
    put(la_ref, log_sig * (1.0 / GLA_TAU), GLA_QK)
    misct = misc.T
    iwt = misct[_MISC_IW:_MISC_IW + _IWT_ROWS, :] * ((IDX_HEADS * IDX_DIM) ** -0.5)
    for j in range(nb):
        iwt_ref[j] = iwt[:, j * rows:(j + 1) * rows]


def _proj(x, mod, g_mix, w_packed, wa2, b_a, gq_t, gk_t, bd, *, tm=512):
    b, l, d = x.shape
    nb, rows = _row_tiling(b, l, tm)
    grid = (b // nb, l // rows)
    row_spec = lambda w: pl.BlockSpec((nb, rows, w), lambda i, r: (i, r, 0))
    col_spec = lambda h: pl.BlockSpec((nb, h, rows), lambda i, r: (i, 0, r))
    sds = lambda w, dt: jax.ShapeDtypeStruct((b, l, w), dt)
    out_shape = [
        sds(2 * GLA_QK, F32), sds(GLA_V, BF16), sds(GLA_V, BF16), sds(GLA_QK, F32), sds(ATT_Q, BF16),
        sds(ATT_KV, F32), sds(ATT_KV, F32), jax.ShapeDtypeStruct((b, ATT_KV, l), BF16),
        sds(IDX_DIM, F32), sds(IDX_Q, F32), jax.ShapeDtypeStruct((b, _IWT_ROWS, l), F32),
    ]
    out_specs = [
        row_spec(2 * GLA_QK), row_spec(GLA_V), row_spec(GLA_V), row_spec(GLA_QK), row_spec(ATT_Q),
        row_spec(ATT_KV), row_spec(ATT_KV), col_spec(ATT_KV),
        row_spec(IDX_DIM), row_spec(IDX_Q), col_spec(_IWT_ROWS),
    ]
    consts = [g_mix, w_packed, wa2, b_a, gq_t, gk_t, bd]
    return pl.pallas_call(
        functools.partial(_proj_kernel, nb=nb, rows=rows),
        out_shape=out_shape,
        grid=grid,
        in_specs=[row_spec(d), pl.BlockSpec((nb, 9, d), lambda i, r: (i, 0, 0))]
        + [_const_spec(c.shape) for c in consts],
        out_specs=out_specs,
        scratch_shapes=[pltpu.VMEM((nb * rows, d), BF16)],
        compiler_params=_params("parallel", "parallel"),
        name="proj",
    )(x, mod, *consts)


def _gla_kernel(*refs, has_s0, nbb, t):
    it = iter(refs)
    qk_ref, v_ref, la_ref, sgr_ref = next(it), next(it), next(it), next(it)
    s0_ref = next(it) if has_s0 else None
    g_ref, bd_ref, o_ref, st_ref, s_scr = next(it), next(it), next(it), next(it), next(it)
    cb = GLA_BLOCK
    nsb = t // cb
    c = pl.program_id(1)

    @pl.when(c == 0)
    def _():
        s_scr[...] = jnp.zeros((nbb, GLA_QK, GLA_V), F32)
        if has_s0:
            for j in range(nbb):
                for h in range(GLA_HEADS):
                    s_scr[j, h * GLA_DK:(h + 1) * GLA_DK, h * GLA_DV:(h + 1) * GLA_DV] = s0_ref[j, h]

    ri = lax.broadcasted_iota(I32, (t, t), 0)
    ci = lax.broadcasted_iota(I32, (t, t), 1)
    tril = jnp.where(ri >= ci, 1.0, 0.0).astype(BF16)
    lane = lax.broadcasted_iota(I32, (1, GLA_QK), 1)
    head_mask = [jnp.where((lane >= h * GLA_DK) & (lane < (h + 1) * GLA_DK), 1.0, 0.0).astype(F32)
                 for h in range(GLA_HEADS)]
    g = g_ref[...]


    cum = []
    for j in range(nbb):
        la = la_ref[j]
        p0 = la.astype(BF16)
        r0 = la - p0.astype(F32)
        p1 = r0.astype(BF16)
        p2 = (r0 - p1.astype(F32)).astype(BF16)
        bb = _dot(tril, jnp.concatenate([p0, p1, p2], axis=1))
        cum.append(bb[:, :GLA_QK] + bb[:, GLA_QK:2 * GLA_QK] + bb[:, 2 * GLA_QK:])

    o_inter, scores = [], []
    for j in range(nbb):
        b = cum[j]
        qk = qk_ref[j]
        q, k = qk[:, :GLA_QK], qk[:, GLA_QK:]
        v = v_ref[j]
        b0 = [jnp.zeros((1, GLA_QK), F32)] + [b[cb * i - 1:cb * i, :] for i in range(1, nsb)]
        btot = b[t - 1:t, :]
        bstart = jnp.concatenate([jnp.broadcast_to(b0[i], (cb, GLA_QK)) for i in range(nsb)], axis=0)
        q_rel = q * jnp.exp(b - bstart)
        q_int = (q * jnp.exp(b)).astype(BF16)
        k_end = k * jnp.exp(btot - b)

        s_old = s_scr[j]
        o_inter.append(_dot(q_int, s_old.astype(BF16)))
        ds = _dot(k_end.T.astype(BF16), v)
        dcol = jnp.exp(jnp.broadcast_to(btot, (V7X_SUBLANES, GLA_QK)).T[:, 0:1])
        s_scr[j] = s_old * dcol + ds * bd_ref[...]

        sc = []
        for i in range(nsb):
            n = cb * (i + 1)
            qs = q_rel[cb * i:cb * (i + 1)]
            q_stack = jnp.concatenate([qs * head_mask[h] for h in range(GLA_HEADS)], axis=0).astype(BF16)
            km = (k[:n] * jnp.exp(b0[i] - b[:n])).astype(BF16)
            sc.append(_dot_nt(q_stack, km))
        scores.append(sc)

    for j in range(nbb):
        v = v_ref[j]
        intra = []
        for i in range(nsb):
            n = cb * (i + 1)
            rr = lax.broadcasted_iota(I32, (GLA_HEADS * cb, n), 0)
            cc = lax.broadcasted_iota(I32, (GLA_HEADS * cb, n), 1)
            a = jnp.where(cc - cb * i <= (rr & (cb - 1)), scores[j][i], 0.0).astype(BF16)
            oi = _dot(a, v[:n])
            intra.append(jnp.concatenate(
                [oi[h * cb:(h + 1) * cb, h * GLA_DV:(h + 1) * GLA_DV] for h in range(GLA_HEADS)], axis=1))
        o = o_inter[j] + jnp.concatenate(intra, axis=0)
        on = jnp.concatenate([_rms(o[:, h * GLA_DV:(h + 1) * GLA_DV]) * g for h in range(GLA_HEADS)], axis=1)
        o_ref[j] = (on * sgr_ref[j].astype(F32)).astype(o_ref.dtype)

    @pl.when(c == pl.num_programs(1) - 1)
    def _():
        for j in range(nbb):
            for h in range(GLA_HEADS):
                st_ref[j, h] = s_scr[j, h * GLA_DK:(h + 1) * GLA_DK, h * GLA_DV:(h + 1) * GLA_DV]


def _gla(qk, v, la, sgr, s0, g_gla, bd_state, *, nbb=8):
    b, l, _ = qk.shape
    t = min(_GLA_T, l)
    assert l % t == 0 and t % GLA_BLOCK == 0
    while b % nbb:
        nbb -= 1
    tok = lambda w: pl.BlockSpec((nbb, t, w), lambda i, c: (i, c, 0))
    st_spec = pl.BlockSpec((nbb, GLA_HEADS, GLA_DK, GLA_DV), lambda i, c: (i, 0, 0, 0))
    in_specs = [tok(2 * GLA_QK), tok(GLA_V), tok(GLA_QK), tok(GLA_V)]
    args = [qk, v, la, sgr]
    if s0 is not None:
        in_specs.append(st_spec)
        args.append(s0)
    in_specs += [_const_spec(g_gla.shape), _const_spec(bd_state.shape)]
    args += [g_gla, bd_state]
    return pl.pallas_call(
        functools.partial(_gla_kernel, has_s0=s0 is not None, nbb=nbb, t=t),
        out_shape=[jax.ShapeDtypeStruct((b, l, GLA_V), BF16),
                   jax.ShapeDtypeStruct((b, GLA_HEADS, GLA_DK, GLA_DV), F32)],
        grid=(b // nbb, l // t),
        in_specs=in_specs,
        out_specs=[tok(GLA_V), st_spec],
        scratch_shapes=[pltpu.VMEM((nbb, GLA_QK, GLA_V), F32)],
        compiler_params=_params("parallel", "arbitrary"),
        name="gla",
    )(*args)


def _dsa_kernel(*refs, p_len, e_new, tq, topk):
    has_past = p_len > 0
    it = iter(refs)
    if has_past:
        kp_ref, vp_ref, ikp_ref = next(it), next(it), next(it)
    kn_ref, vtn_ref, ikn_ref, q_ref, iq_ref, iwt_ref = (next(it) for _ in range(6))
    o_ref, score_scr, bias_scr = (next(it) for _ in range(3))
    e = p_len + e_new
    w0 = e - tq

    k_all, ik_all, vt = kn_ref[0], ikn_ref[0], vtn_ref[0]
    if has_past:
        k_all = jnp.concatenate([kp_ref[0], k_all], axis=0)
        ik_all = jnp.concatenate([ikp_ref[0], ik_all], axis=0)
        vt = jnp.concatenate([vp_ref[0].T.astype(BF16), vt], axis=1)

    def on_window(full, fn):
        return jnp.concatenate([full[:w0], fn(full[w0:])], axis=0) if w0 else fn(full)

    wr = lax.broadcasted_iota(I32, (tq, tq), 0)
    wc = lax.broadcasted_iota(I32, (tq, tq), 1)
    limit = (((w0 + wc) >> _CHUNK_SHIFT) + 1) << _CHUNK_SHIFT
    adm_win = (w0 + wr) < limit

    ik_hi = ik_all.astype(BF16)
    ik_lo = (ik_all - ik_hi.astype(F32)).astype(BF16)
    ik3 = jnp.concatenate([ik_hi, ik_lo, ik_hi], axis=1)
    iq = iq_ref[0]
    iwt = iwt_ref[0]
    wide = tq % V7X_LANES == 0

    def iq_pieces(h):
        iqh = iq[:, h * IDX_DIM:(h + 1) * IDX_DIM]
        hi = iqh.astype(BF16)
        lo = (iqh - hi.astype(F32)).astype(BF16)
        return jnp.concatenate([hi, hi, lo], axis=1)

    if wide:
        logits = _dot_nt(ik3, jnp.concatenate([iq_pieces(h) for h in range(IDX_HEADS)], axis=0))
        logit = [logits[:, h * tq:(h + 1) * tq] for h in range(IDX_HEADS)]
    else:
        logit = [_dot_nt(ik3, iq_pieces(h)) for h in range(IDX_HEADS)]
    score = jnp.zeros((e, tq), F32)
    for h in range(IDX_HEADS):
        score = score + iwt[h:h + 1, :] * jnp.maximum(logit[h], 0.0)
    score_scr[...] = on_window(score, lambda s: jnp.where(adm_win, s, -jnp.inf))

    def key_to_f32(c):
        return lax.bitcast_convert_type(c ^ ((c >> 31) & 0x7FFFFFFF), F32)

    def count_ge(cand):
        cf = key_to_f32(cand)
        slab = 8 * V7X_SUBLANES
        if e % slab:
            return _col_reduce(jnp.where(score_scr[...] >= cf, 1.0, 0.0), jnp.sum)
        acc = jnp.zeros((slab, tq), F32)
        for i in range(e // slab):
            acc = jnp.where(score_scr[i * slab:(i + 1) * slab, :] >= cf, acc + 1.0, acc)
        return jnp.sum(acc, axis=0, keepdims=True)

    c0 = count_ge(jnp.zeros((1, tq), I32))
    thr = jnp.where(c0 >= topk, 0, _INT_MIN).astype(I32)
    cnt = jnp.where(c0 >= topk, c0, float(e))

    def thr_step(i, carry):
        thr, cnt = carry
        cand = thr | lax.shift_left(jnp.int32(1), 30 - i)
        c = count_ge(cand)
        ok = c >= topk
        return jnp.where(ok, cand, thr), jnp.where(ok, c, cnt)

    thr, cnt = lax.fori_loop(0, 31, thr_step, (thr, cnt))
    thr_f = jnp.where(thr == _INT_MIN, -jnp.inf, key_to_f32(thr))

    sc = score_scr[...]
    blk = next(b for b in (256, 128, 64, 32, 16, 8) if e % b == 0)
    ri = lax.broadcasted_iota(I32, (blk, blk), 0)
    ci = lax.broadcasted_iota(I32, (blk, blk), 1)
    tri = jnp.where(ri >= ci, 1.0, 0.0).astype(BF16)
    n_tied = jnp.zeros((1, tq), F32)
    ranks = []
    for j in range(e // blk):
        tied = jnp.where(sc[j * blk:(j + 1) * blk] == thr_f, 1.0, 0.0).astype(BF16)
        ranks.append(_dot(tri, tied) + n_tied)
        n_tied = ranks[-1][blk - 1:blk, :]
    need = topk - (cnt - n_tied)
    sel_bias = jnp.concatenate([
        jnp.where(sc[j * blk:(j + 1) * blk] > thr_f, 0.0,
                  jnp.where(sc[j * blk:(j + 1) * blk] == thr_f, jnp.where(ranks[j] <= need, 0.0, _NEG), _NEG))
        for j in range(e // blk)], axis=0)
    bias_scr[...] = on_window(sel_bias, lambda s: jnp.where(adm_win, s, _NEG))

    assert e <= _POS_SPLIT * _POS_SPLIT
    pr = lax.broadcasted_iota(I32, (e, ATT_HD), 0)
    pc = lax.broadcasted_iota(I32, (e, ATT_HD), 1)
    pos_lo = pr & (_POS_SPLIT - 1)
    pos_cols = jnp.where(pc < 2 * _SLOPE_PIECES, jnp.where((pc & 1) == 0, pr - pos_lo, pos_lo), 0)
    pos_cols = pos_cols.astype(F32).astype(BF16)
    k_aug = [jnp.concatenate([k_all[:, g * ATT_HD:(g + 1) * ATT_HD].astype(BF16), pos_cols], axis=1)
             for g in range(ATT_KV_HEADS)]
    ahead = -2.0 * jnp.maximum(wr - wc, 0).astype(F32)
    qc = lax.broadcasted_iota(I32, (tq, ATT_HD), 1)
    q = q_ref[0]

    def slope_of(h):
        return jnp.float32(_LOG2E * 2.0 ** (-8.0 * (h + 1) / ATT_HEADS))

    def q_aug(h):
        rest = jnp.full((tq, ATT_HD), slope_of(h), F32)
        cols = jnp.zeros((tq, ATT_HD), F32)
        for i in range(_SLOPE_PIECES):
            piece = rest.astype(BF16).astype(F32)
            rest = rest - piece
            cols = jnp.where((qc >> 1) == i, piece, cols)
        return jnp.concatenate([q[:, h * ATT_HD:(h + 1) * ATT_HD], cols.astype(BF16)], axis=1)

    def probs_of(s, h):
        s = on_window(s, lambda sw: sw + slope_of(h) * ahead) + bias_scr[...]
        return jnp.exp2(s - _col_reduce(s, jnp.max)).astype(BF16)

    ones_rows = jnp.ones((_BF16_SUBLANES, e), BF16)
    vt_aug = [jnp.concatenate([vt[g * ATT_HD:(g + 1) * ATT_HD, :], ones_rows], axis=0) for g in range(ATT_KV_HEADS)]

    def normalised(ot):
        return ot[:ATT_HD] / ot[ATT_HD:ATT_HD + 1]

    outs = []
    if wide:
        group_logits = [
            _dot_nt(k_aug[g], jnp.concatenate([q_aug(g * ATT_REP + r) for r in range(ATT_REP)], axis=0))
            for g in range(ATT_KV_HEADS)]
        for g in range(ATT_KV_HEADS):
            probs = [probs_of(group_logits[g][:, r * tq:(r + 1) * tq], g * ATT_REP + r) for r in range(ATT_REP)]
            ot = _dot(vt_aug[g], jnp.concatenate(probs, axis=1))
            outs += [normalised(ot[:, r * tq:(r + 1) * tq]) for r in range(ATT_REP)]
    else:
        ahead_heads = max(1, min(ATT_HEADS - 1, _DSA_LOGIT_ROWS_IN_FLIGHT // e))
        qk = lambda h: _dot_nt(k_aug[h // ATT_REP], q_aug(h))
        pending = [qk(h) for h in range(ahead_heads)]
        for h in range(ATT_HEADS):
            s = pending.pop(0)
            if h + ahead_heads < ATT_HEADS:
                pending.append(qk(h + ahead_heads))
            outs.append(normalised(_dot(vt_aug[h // ATT_REP], probs_of(s, h))))
    ot = jnp.concatenate(outs, axis=0)
    if tq % V7X_LANES:
        ot = jnp.concatenate([ot, jnp.zeros((ATT_Q, V7X_LANES - tq % V7X_LANES), F32)], axis=1)
    o_ref[0] = ot.T[:tq].astype(o_ref.dtype)


def _dsa_call(q_att, iq, iwt, k_new, vt_new, ik_new, past, *, tq, tile, topk):
    b = q_att.shape[0]
    p_len = 0 if past is None else past[0].shape[1]
    e_new = (tile + 1) * tq
    e = p_len + e_new
    in_specs, args = [], []
    if past is not None:
        in_specs += [pl.BlockSpec((1, p_len, w), lambda i: (i, 0, 0)) for w in (ATT_KV, ATT_KV, IDX_DIM)]
        args += list(past)
    in_specs += [
        pl.BlockSpec((1, e_new, ATT_KV), lambda i: (i, 0, 0)),
        pl.BlockSpec((1, ATT_KV, e_new), lambda i: (i, 0, 0)),
        pl.BlockSpec((1, e_new, IDX_DIM), lambda i: (i, 0, 0)),
        pl.BlockSpec((1, tq, ATT_Q), lambda i: (i, tile, 0)),
        pl.BlockSpec((1, tq, IDX_Q), lambda i: (i, tile, 0)),
        pl.BlockSpec((1, _IWT_ROWS, tq), lambda i: (i, 0, tile)),
    ]
    args += [k_new, vt_new, ik_new, q_att, iq, iwt]
    kern = functools.partial(_dsa_kernel, p_len=p_len, e_new=e_new, tq=tq, topk=topk)
    return pl.pallas_call(
        kern,
        out_shape=jax.ShapeDtypeStruct((b, tq, ATT_Q), BF16),
        grid=(b,),
        in_specs=in_specs,
        out_specs=pl.BlockSpec((1, tq, ATT_Q), lambda i: (i, 0, 0)),
        scratch_shapes=[
            pltpu.VMEM((e, tq), F32),
            pltpu.VMEM((e, tq), F32),
        ],
        compiler_params=_params("parallel"),
        name=f"dsa_e{e}",
    )(*args)


def _dsa(q_att, iq, iwt, k_new, vt_new, ik_new, past):
    b, l, _ = q_att.shape
    p_len = 0 if past is None else past[0].shape[1]
    topk = min(TOPK_MAX, (p_len + l) // 4)
    if l <= CHUNK:
        return _dsa_call(q_att, iq, iwt, k_new, vt_new, ik_new, past, tq=l, tile=0, topk=topk)
    assert l % _DSA_TQ == 0
    outs = [_dsa_call(q_att, iq, iwt, k_new, vt_new, ik_new, past, tq=_DSA_TQ, tile=t, topk=topk)
            for t in range(l // _DSA_TQ)]
    return jnp.concatenate(outs, axis=1)


def _block_diag_ones(n, blk):
    i = jnp.arange(n) // blk
    return (i[:, None] == i[None, :])


def _prep_weights(w_ada, b_ada, g_ffn1, w1_gate, w1_up, w1_down, g_mix, w_in, w_a2, b_a, g_gla,
                  g_q, g_k, w_out, g_ffn2, w2_gate, w2_up, w2_down, g_out):
    sizes = (GLA_QK, GLA_QK, GLA_V, GLA_LOWRANK, GLA_V, ATT_Q, ATT_KV, ATT_KV, IDX_Q, IDX_DIM, IDX_HEADS)
    pts, acc = [], 0
    for s in sizes[:-1]:
        acc += s
        pts.append(acc)
    gq, gk, gv, ga, gr, aq, ak, av, iq, ik, iw = jnp.split(w_in, pts, axis=1)
    pad = jnp.zeros((D_MODEL, _PACKED_WIDTH - sum(sizes)), w_in.dtype)
    w_packed = jnp.concatenate([gq, gk, gv, gr, aq, ak, av, iq, ik, ga, iw, pad], axis=1).astype(BF16)
    wa2 = jnp.zeros((V7X_LANES, GLA_QK), F32).at[_MISC_GA:_MISC_GA + GLA_LOWRANK].set(w_a2).astype(BF16)
    row = lambda a: a.reshape(1, -1).astype(F32)
    bd_state = jnp.repeat(jnp.repeat(jnp.eye(GLA_HEADS, dtype=F32), GLA_DK, axis=0), GLA_DV, axis=1)
    return dict(
        w_ada=w_ada.astype(BF16), b_ada=row(b_ada),
        g_ffn1=row(g_ffn1), w1=(w1_gate.astype(BF16), w1_up.astype(BF16), w1_down.astype(BF16)),
        g_mix=row(g_mix), w_packed=w_packed, wa2=wa2, b_a=row(b_a),
        g_gla=row(g_gla), gq_t=row(jnp.tile(g_q, ATT_HEADS)), gk_t=row(jnp.tile(g_k, ATT_KV_HEADS)),
        bd_heads=_block_diag_ones(ATT_Q, ATT_HD).astype(BF16), bd_state=bd_state,
        w_out=w_out.astype(BF16),
        g_ffn2=row(g_ffn2), w2=(w2_gate.astype(BF16), w2_up.astype(BF16), w2_down.astype(BF16)),
        g_out=row(g_out),
    )


def _layer(x, mod, past, s0, w):
    b, l, _ = x.shape
    x1 = _ffn(x, mod, w["g_ffn1"], *w["w1"], mod_base=0)
    (qk, gv, sgr, la, q_att, k_new, v_new, vt_new, ik_new, iq, iwt) = _proj(
        x1, mod, w["g_mix"], w["w_packed"], w["wa2"], w["b_a"], w["gq_t"], w["gk_t"], w["bd_heads"])
    gla_out, s_t = _gla(qk, gv, la, sgr, s0, w["g_gla"], w["bd_state"])
    att = _dsa(q_att, iq, iwt, k_new, vt_new, ik_new, past)
    y = _ffn(x1, mod, w["g_ffn2"], *w["w2"], mod_base=6, mix=(gla_out, att, w["w_out"]), g_out=w["g_out"])
    kv_shape = (b, l, ATT_KV_HEADS, ATT_HD)
    return y, k_new.reshape(kv_shape), v_new.reshape(kv_shape), ik_new, s_t


def kernel(x_prompt, x_sample, c_prompt, c_sample, cache_k, cache_v, cache_idx_k, state_gla, w_ada, b_ada, g_ffn1, w1_gate, w1_up, w1_down, g_mix, w_in, w_a2, b_a, g_gla, g_q, g_k, w_out, g_ffn2, w2_gate, w2_up, w2_down, g_out):
    depth = w_ada.shape[0]
    bp = x_prompt.shape[0]
    yp, ys = x_prompt, x_sample
    outs_p, outs_s = [], []
    for layer in range(depth):
        w = _prep_weights(*(t[layer] for t in (
            w_ada, b_ada, g_ffn1, w1_gate, w1_up, w1_down, g_mix, w_in, w_a2, b_a, g_gla,
            g_q, g_k, w_out, g_ffn2, w2_gate, w2_up, w2_down, g_out)))
        mod = _adaln(jnp.concatenate([c_prompt, c_sample], axis=0), w["w_ada"], w["b_ada"])
        mod = mod.reshape(mod.shape[0], 9, D_MODEL)
        ds, pp = cache_k.shape[1], cache_k.shape[2]
        past = (cache_k[layer].reshape(ds, pp, ATT_KV), cache_v[layer].reshape(ds, pp, ATT_KV), cache_idx_k[layer])
        yp, *rest_p = _layer(yp, mod[:bp], None, None, w)
        ys, *rest_s = _layer(ys, mod[bp:], past, state_gla[layer], w)
        outs_p.append(rest_p)
        outs_s.append(rest_s)
    stack = lambda outs, i: jnp.stack([o[i] for o in outs])
    return (yp, ys,
            stack(outs_p, 0), stack(outs_p, 1), stack(outs_p, 2), stack(outs_p, 3),
            stack(outs_s, 0), stack(outs_s, 1), stack(outs_s, 2), stack(outs_s, 3))
```
